```python
import math
import jax
import jax.numpy as jnp
from jax import lax
import numpy as np

D_MODEL = 1024
BATCH = 16
SEQ = 4096
DEPTH = 4
DEC_BATCH = 4
DEC_SEQ = 8192
PAST_LEN = 128

HEAD_DIM = 64
D_FF = -(-(8 * D_MODEL) // (3 * 256)) * 256
N_EVEN = (DEPTH + 1) // 2
N_ODD = DEPTH // 2
D_SSD = D_MODEL
SSD_HEADS = D_SSD // HEAD_DIM
SSD_GROUPS = 2
SSD_HG = SSD_HEADS // SSD_GROUPS
SSD_STATE = 128
SSD_CONV = 5
SSD_CHUNK = 128
SSD_CONV_CH = D_SSD + 2 * SSD_GROUPS * SSD_STATE
D_RWKV = D_MODEL
RWKV_HEADS = D_RWKV // HEAD_DIM
DECAY_RANK = 64
ICL_RANK = 64
GATE_RANK = 128
RWKV_PROJ = 3 * D_RWKV + DECAY_RANK + ICL_RANK + GATE_RANK
RWKV_LN_EPS = 64e-5
D_CONV = D_MODEL
CONV_WIDTH = 31
ATT_Q_HEADS = 16
ATT_KV_HEADS = 4
ATT_GQ = ATT_Q_HEADS // ATT_KV_HEADS
WINDOW = 128
ATT_BLOCK = 128
REL_BUCKETS = 32
REL_MAX_DIST = 128
EV_IN = D_SSD + SSD_CONV_CH + 2 * SSD_HEADS + RWKV_PROJ
EV_OUT = D_SSD + D_RWKV
OD_IN = 2 * D_CONV + (ATT_Q_HEADS + 2 * ATT_KV_HEADS) * HEAD_DIM
OD_OUT = D_CONV + ATT_Q_HEADS * HEAD_DIM

kernel_name = 'hybrid_ssd_rwkv_conformer_swa_encoder'


def _rms_norm(x, w, eps=1e-6):
    xf = x.astype(jnp.float32)
    xf = xf * lax.rsqrt(jnp.mean(xf * xf, axis=-1, keepdims=True) + eps)
    return (xf * w.astype(jnp.float32)).astype(x.dtype)


def _group_rms_norm(x, w, groups, eps=1e-6):
    shp = x.shape
    xf = x.astype(jnp.float32).reshape(shp[:-1] + (groups, shp[-1] // groups))
    xf = xf * lax.rsqrt(jnp.mean(xf * xf, axis=-1, keepdims=True) + eps)
    return xf.reshape(shp) * w.astype(jnp.float32)


def _group_layer_norm(x, w, b, groups, eps):
    shp = x.shape
    xf = x.astype(jnp.float32).reshape(shp[:-1] + (groups, shp[-1] // groups))
    xc = xf - jnp.mean(xf, axis=-1, keepdims=True)
    xn = xc * lax.rsqrt(jnp.mean(xc * xc, axis=-1, keepdims=True) + eps)
    return xn.reshape(shp) * w.astype(jnp.float32) + b.astype(jnp.float32)


def _dw_conv(x, w, b):
    k = w.shape[0]
    y = lax.conv_general_dilated(x, w[:, None, :], window_strides=(1,), padding=[(k // 2, k // 2)],
                                 dimension_numbers=('NWC', 'WIO', 'NWC'), feature_group_count=x.shape[-1])
    return y + b


def _segsum(a):
    t = a.shape[-1]
    cs = jnp.cumsum(a, axis=-1)
    diff = cs[..., :, None] - cs[..., None, :]
    return jnp.where(jnp.tril(jnp.ones((t, t), dtype=bool)), diff, -jnp.inf)


def _ssd_chunked(x, a, b, c):
    bsz, t = x.shape[:2]
    nc = t // SSD_CHUNK
    x = x.reshape(bsz, nc, SSD_CHUNK, SSD_GROUPS, SSD_HG, HEAD_DIM)
    b = b.reshape(bsz, nc, SSD_CHUNK, SSD_GROUPS, SSD_STATE)
    c = c.reshape(bsz, nc, SSD_CHUNK, SSD_GROUPS, SSD_STATE)
    a = a.reshape(bsz, nc, SSD_CHUNK, SSD_GROUPS, SSD_HG).transpose(0, 3, 4, 1, 2)
    a_cs = jnp.cumsum(a, axis=-1)
    decay_in = jnp.exp(_segsum(a))
    cb = jnp.einsum('bclgn,bcsgn->bcgls', c, b)
    y_diag = jnp.einsum('bcgls,bghcls,bcsghp->bclghp', cb, decay_in, x)
    decay_states = jnp.exp(a_cs[..., -1:] - a_cs)
    states = jnp.einsum('bclgn,bghcl,bclghp->bcghpn', b, decay_states, x)
    states = jnp.concatenate([jnp.zeros_like(states[:, :1]), states], axis=1)
    chunk_tot = jnp.pad(a_cs[..., -1], ((0, 0), (0, 0), (0, 0), (1, 0)))
    decay_chunk = jnp.exp(_segsum(chunk_tot))
    states = jnp.einsum('bghzc,bcghpn->bzghpn', decay_chunk, states)[:, :-1]
    y_off = jnp.einsum('bclgn,bcghpn,bghcl->bclghp', c, states, jnp.exp(a_cs))
    return (y_diag + y_off).reshape(bsz, t, SSD_GROUPS, SSD_HG, HEAD_DIM)


def _ssd_mixer(z, xbc, dt_raw, conv_w, conv_b, dt_bias, a_log, d_skip, norm_w):
    bsz, t = z.shape[:2]
    xbc = jax.nn.silu(_dw_conv(xbc, conv_w, conv_b)).astype(jnp.float32)
    xs, bs, cs = jnp.split(xbc, [D_SSD, D_SSD + SSD_GROUPS * SSD_STATE], axis=-1)
    xs = xs.reshape(bsz, t, SSD_GROUPS, SSD_HG, HEAD_DIM)
    bs = bs.reshape(bsz, t, SSD_GROUPS, SSD_STATE)
    cs = cs.reshape(bsz, t, SSD_GROUPS, SSD_STATE)
    dt = jax.nn.softplus((dt_raw.reshape(bsz, t, 2, SSD_HEADS) + dt_bias).astype(jnp.float32))
    da = (dt * -jnp.exp(a_log.astype(jnp.float32))).reshape(bsz, t, 2, SSD_GROUPS, SSD_HG)
    dt = dt.reshape(bsz, t, 2, SSD_GROUPS, SSD_HG)
    y_fwd = _ssd_chunked(xs * dt[:, :, 0, :, :, None], da[:, :, 0], bs, cs)
    flip = lambda u: jnp.flip(u, axis=1)
    y_bwd = flip(_ssd_chunked(flip(xs * dt[:, :, 1, :, :, None]), flip(da[:, :, 1]), flip(bs), flip(cs)))
    y = y_fwd + y_bwd + d_skip.astype(jnp.float32).reshape(SSD_GROUPS, SSD_HG, 1) * xs
    y = y.reshape(bsz, t, D_SSD) * jax.nn.silu(z.astype(jnp.float32))
    return _group_rms_norm(y, norm_w, SSD_GROUPS).astype(z.dtype)


def _delta_scan(r, w, k, v, a, b, reverse):
    def step(s, inp):
        r_t, w_t, k_t, v_t, a_t, b_t = inp
        sa = jnp.einsum('bhij,bhj->bhi', s, a_t)
        s = s * w_t[:, :, None, :] + sa[..., None] * b_t[:, :, None, :] + v_t[..., None] * k_t[:, :, None, :]
        return s, jnp.einsum('bhij,bhj->bhi', s, r_t)
    s0 = jnp.zeros(r.shape[1:] + (r.shape[-1],), jnp.float32)
    _, y = lax.scan(step, s0, (r, w, k, v, a, b), reverse=reverse)
    return y


def _rwkv_mixer(p, mu, w0, w2, a0, a2, g2, k_k, k_a, r_k, ln_w, ln_b):
    bsz, t = p.shape[:2]
    pf = p.astype(jnp.float32)
    prev = jnp.pad(pf[:, :-1], ((0, 0), (1, 0), (0, 0)))
    nxt = jnp.pad(pf[:, 1:], ((0, 0), (0, 1), (0, 0)))
    pf = pf + mu.astype(jnp.float32) * (0.5 * (prev + nxt) - pf)
    r, k, v, cw, ca, cg = jnp.split(pf, [D_RWKV, 2 * D_RWKV, 3 * D_RWKV, 3 * D_RWKV + DECAY_RANK,
                                          3 * D_RWKV + DECAY_RANK + ICL_RANK], axis=-1)
    a = jax.nn.sigmoid(a0.astype(jnp.float32) + ca @ a2.astype(jnp.float32))
    g = jax.nn.sigmoid(cg) @ g2.astype(jnp.float32)
    wlog = -jax.nn.softplus(-(w0.astype(jnp.float32)[:, None, None, :]
                              + jnp.einsum('btr,drc->dbtc', jnp.tanh(cw), w2.astype(jnp.float32)))) - 0.5
    decay = jnp.exp(-jnp.exp(wlog))
    kk = (k * k_k.astype(jnp.float32)).reshape(bsz, t, RWKV_HEADS, HEAD_DIM)
    kk = kk / jnp.maximum(jnp.sqrt(jnp.sum(kk * kk, axis=-1, keepdims=True)), 1e-12)
    k = k * (1.0 + (a - 1.0) * k_a.astype(jnp.float32))
    heads = lambda u: u.reshape(bsz, t, RWKV_HEADS, HEAD_DIM).transpose(1, 0, 2, 3)
    rs, ks, vs, a_h = heads(r), heads(k), heads(v), heads(a)
    kk_t = kk.transpose(1, 0, 2, 3)
    ia, ib = -kk_t, kk_t * a_h
    y_fwd = _delta_scan(rs, heads(decay[0]), ks, vs, ia, ib, reverse=False)
    y_bwd = _delta_scan(rs, heads(decay[1]), ks, vs, ia, ib, reverse=True)
    y = (y_fwd + y_bwd).transpose(1, 0, 2, 3).reshape(bsz, t, D_RWKV)
    y = _group_layer_norm(y, ln_w, ln_b, RWKV_HEADS, RWKV_LN_EPS)
    bonus = jnp.sum((r * k).reshape(bsz, t, RWKV_HEADS, HEAD_DIM) * r_k.astype(jnp.float32), axis=-1, keepdims=True)
    y = (y + (bonus * v.reshape(bsz, t, RWKV_HEADS, HEAD_DIM)).reshape(bsz, t, D_RWKV)) * g
    return y.astype(p.dtype)


def _even_mixer(h, w_in, w_out, ssd_conv_w, ssd_conv_b, ssd_dt_bias, ssd_a_log, ssd_d, ssd_norm_w,
                rwkv_mu, rwkv_w0, rwkv_w2, rwkv_a0, rwkv_a2, rwkv_g2, rwkv_k_k, rwkv_k_a, rwkv_r_k,
                rwkv_ln_w, rwkv_ln_b):
    u = h @ w_in
    z, xbc, dt_raw, p_rwkv = jnp.split(u, [D_SSD, D_SSD + SSD_CONV_CH, D_SSD + SSD_CONV_CH + 2 * SSD_HEADS], axis=-1)
    y_a = _ssd_mixer(z, xbc, dt_raw, ssd_conv_w, ssd_conv_b, ssd_dt_bias, ssd_a_log, ssd_d, ssd_norm_w)
    y_b = _rwkv_mixer(p_rwkv, rwkv_mu, rwkv_w0, rwkv_w2, rwkv_a0, rwkv_a2, rwkv_g2, rwkv_k_k, rwkv_k_a,
                      rwkv_r_k, rwkv_ln_w, rwkv_ln_b)
    return jnp.concatenate([y_a, y_b], axis=-1) @ w_out


def _conv_module(c_in, dw_w, dw_b, ln_w, ln_b):
    val, gate = jnp.split(c_in, 2, axis=-1)
    u = val * jax.nn.sigmoid(gate)
    u = _dw_conv(u, dw_w, dw_b)
    u = _group_layer_norm(u, ln_w, ln_b, 1, 1e-5)
    return jax.nn.silu(u).astype(c_in.dtype)


def _t5_bucket(rel):
    nb = REL_BUCKETS // 2
    max_exact = nb // 2
    n = jnp.abs(rel)
    nf = jnp.maximum(n, 1).astype(jnp.float32)
    large = max_exact + (jnp.log(nf / max_exact) / math.log(REL_MAX_DIST / max_exact)
                         * (nb - max_exact)).astype(jnp.int32)
    large = jnp.minimum(large, nb - 1)
    return (rel > 0).astype(jnp.int32) * nb + jnp.where(n < max_exact, n, large)


def _window_attention(q, k, v, q_norm_w, k_norm_w, sink, rel_bias):
    bsz, t = q.shape[:2]
    nblk = t // ATT_BLOCK
    q = _rms_norm(q, q_norm_w)
    k = _rms_norm(k, k_norm_w)
    qb = q.reshape(bsz, nblk, ATT_BLOCK, ATT_KV_HEADS, ATT_GQ, HEAD_DIM).transpose(1, 0, 2, 3, 4, 5)
    pad = ((0, 0), (ATT_BLOCK, ATT_BLOCK), (0, 0), (0, 0))
    kp, vp = jnp.pad(k, pad), jnp.pad(v, pad)
    rel = jnp.arange(3 * ATT_BLOCK)[None, :] - ATT_BLOCK - jnp.arange(ATT_BLOCK)[:, None]
    bias = rel_bias.astype(jnp.float32)[_t5_bucket(rel)]
    bias = bias.transpose(2, 0, 1).reshape(ATT_KV_HEADS, ATT_GQ, ATT_BLOCK, 3 * ATT_BLOCK)
    inband = jnp.abs(rel) <= WINDOW
    sink_l = sink.astype(jnp.float32).reshape(ATT_KV_HEADS, ATT_GQ, 1, 1)
    scale = HEAD_DIM ** -0.5

    def block(args):
        q_blk, i = args
        start = i * ATT_BLOCK
        k_blk = lax.dynamic_slice_in_dim(kp, start, 3 * ATT_BLOCK, axis=1)
        v_blk = lax.dynamic_slice_in_dim(vp, start, 3 * ATT_BLOCK, axis=1)
        kpos = start - ATT_BLOCK + jnp.arange(3 * ATT_BLOCK)
        mask = inband & ((kpos >= 0) & (kpos < t))[None, :]
        logits = jnp.einsum('bqkgd,bskd->bkgqs', q_blk, k_blk).astype(jnp.float32) * scale + bias
        logits = jnp.where(mask, logits, -jnp.inf)
        m = jnp.maximum(jnp.max(logits, axis=-1, keepdims=True), sink_l)
        p = jnp.exp(logits - m)
        denom = jnp.sum(p, axis=-1, keepdims=True) + jnp.exp(sink_l - m)
        o = jnp.einsum('bkgqs,bskd->bkgqd', p.astype(v.dtype), v_blk).astype(jnp.float32) / denom
        return o.astype(v.dtype).transpose(0, 3, 1, 2, 4)

    out = lax.map(block, (qb, jnp.arange(nblk, dtype=jnp.int32)))
    return out.transpose(1, 0, 2, 3, 4, 5).reshape(bsz, t, ATT_Q_HEADS * HEAD_DIM)


def _odd_mixer(h, w_in, w_out, conv_dw_w, conv_dw_b, conv_ln_w, conv_ln_b, att_q_norm_w, att_k_norm_w,
               att_sink, rel_bias):
    bsz, t = h.shape[:2]
    u = h @ w_in
    c_in, q, k, v = jnp.split(u, [2 * D_CONV, 2 * D_CONV + ATT_Q_HEADS * HEAD_DIM,
                                  2 * D_CONV + (ATT_Q_HEADS + ATT_KV_HEADS) * HEAD_DIM], axis=-1)
    y_c = _conv_module(c_in, conv_dw_w, conv_dw_b, conv_ln_w, conv_ln_b)
    y_d = _window_attention(q.reshape(bsz, t, ATT_Q_HEADS, HEAD_DIM), k.reshape(bsz, t, ATT_KV_HEADS, HEAD_DIM),
                            v.reshape(bsz, t, ATT_KV_HEADS, HEAD_DIM), att_q_norm_w, att_k_norm_w, att_sink, rel_bias)
    return jnp.concatenate([y_c, y_d], axis=-1) @ w_out


def _swiglu(h, w_in, w_out):
    gate, up = jnp.split(h @ w_in, 2, axis=-1)
    return (jax.nn.silu(gate) * up) @ w_out


def _trunk(x, rel_bias, norm_mix_w, norm_ffn_w, ffn_w_in, ffn_w_out, ev_w_in, ev_w_out, ssd_conv_w, ssd_conv_b,
           ssd_dt_bias, ssd_a_log, ssd_d, ssd_norm_w, rwkv_mu, rwkv_w0, rwkv_w2, rwkv_a0, rwkv_a2, rwkv_g2,
           rwkv_k_k, rwkv_k_a, rwkv_r_k, rwkv_ln_w, rwkv_ln_b, od_w_in, od_w_out, conv_dw_w, conv_dw_b,
           conv_ln_w, conv_ln_b, att_q_norm_w, att_k_norm_w, att_sink):
    for layer in range(DEPTH):
        h = _rms_norm(x, norm_mix_w[layer])
        if layer % 2 == 0:
            e = layer // 2
            x = x + _even_mixer(h, ev_w_in[e], ev_w_out[e], ssd_conv_w[e], ssd_conv_b[e], ssd_dt_bias[e],
                                ssd_a_log[e], ssd_d[e], ssd_norm_w[e], rwkv_mu[e], rwkv_w0[e], rwkv_w2[e],
                                rwkv_a0[e], rwkv_a2[e], rwkv_g2[e], rwkv_k_k[e], rwkv_k_a[e], rwkv_r_k[e],
                                rwkv_ln_w[e], rwkv_ln_b[e])
        else:
            o = layer // 2
            x = x + _odd_mixer(h, od_w_in[o], od_w_out[o], conv_dw_w[o], conv_dw_b[o], conv_ln_w[o], conv_ln_b[o],
                               att_q_norm_w[o], att_k_norm_w[o], att_sink[o], rel_bias)
        x = x + _swiglu(_rms_norm(x, norm_ffn_w[layer]), ffn_w_in[layer], ffn_w_out[layer])
    return x


def setup_inputs(seed: int = 0) -> dict:
    key = jax.random.key(seed)
    ks = iter(jax.random.split(key, 48))
    nrm = lambda shape, scale: scale * jax.random.normal(next(ks), shape, jnp.float32)
    uni = lambda shape, lo, hi: jax.random.uniform(next(ks), shape, jnp.float32, lo, hi)
    E, O = N_EVEN, N_ODD
    dt0 = jnp.exp(uni((E, 2, SSD_HEADS), math.log(1e-3), math.log(1e-1)))
    return {
        'x_prompt': nrm((BATCH, SEQ, D_MODEL), 1.0),
        'x_sample': nrm((DEC_BATCH, DEC_SEQ, D_MODEL), 1.0),
        'rel_bias': nrm((REL_BUCKETS, ATT_Q_HEADS), 0.5),
        'norm_mix_w': 1.0 + nrm((DEPTH, D_MODEL), 0.02),
        'norm_ffn_w': 1.0 + nrm((DEPTH, D_MODEL), 0.02),
        'ffn_w_in': nrm((DEPTH, D_MODEL, 2 * D_FF), D_MODEL ** -0.5),
        'ffn_w_out': nrm((DEPTH, D_FF, D_MODEL), D_FF ** -0.5),
        'ev_w_in': nrm((E, D_MODEL, EV_IN), D_MODEL ** -0.5),
        'ev_w_out': nrm((E, EV_OUT, D_MODEL), EV_OUT ** -0.5),
        'ssd_conv_w': nrm((E, SSD_CONV, SSD_CONV_CH), SSD_CONV ** -0.5),
        'ssd_conv_b': nrm((E, SSD_CONV_CH), 0.02),
        'ssd_dt_bias': dt0 + jnp.log(-jnp.expm1(-dt0)),
        'ssd_a_log': jnp.log(uni((E, 2, SSD_HEADS), 1.0, 16.0)),
        'ssd_d': 1.0 + nrm((E, SSD_HEADS), 0.1),
        'ssd_norm_w': 1.0 + nrm((E, D_SSD), 0.02),
        'rwkv_mu': uni((E, RWKV_PROJ), 0.0, 1.0),
        'rwkv_w0': uni((E, 2, D_RWKV), -6.0, -1.0),
        'rwkv_w2': nrm((E, 2, DECAY_RANK, D_RWKV), 0.5 * DECAY_RANK ** -0.5),
        'rwkv_a0': nrm((E, D_RWKV), 0.1),
        'rwkv_a2': nrm((E, ICL_RANK, D_RWKV), ICL_RANK ** -0.5),
        'rwkv_g2': nrm((E, GATE_RANK, D_RWKV), GATE_RANK ** -0.5),
        'rwkv_k_k': 0.85 + nrm((E, D_RWKV), 0.05),
        'rwkv_k_a': 1.0 + nrm((E, D_RWKV), 0.05),
        'rwkv_r_k': nrm((E, RWKV_HEADS, HEAD_DIM), 0.1),
        'rwkv_ln_w': 1.0 + nrm((E, D_RWKV), 0.02),
        'rwkv_ln_b': nrm((E, D_RWKV), 0.02),
        'od_w_in': nrm((O, D_MODEL, OD_IN), D_MODEL ** -0.5),
        'od_w_out': nrm((O, OD_OUT, D_MODEL), OD_OUT ** -0.5),
        'conv_dw_w': nrm((O, CONV_WIDTH, D_CONV), CONV_WIDTH ** -0.5),
        'conv_dw_b': nrm((O, D_CONV), 0.02),
        'conv_ln_w': 1.0 + nrm((O, D_CONV), 0.02),
        'conv_ln_b': nrm((O, D_CONV), 0.02),
        'att_q_norm_w': 1.0 + nrm((O, HEAD_DIM), 0.02),
        'att_k_norm_w': 1.0 + nrm((O, HEAD_DIM), 0.02),
        'att_sink': nrm((O, ATT_Q_HEADS), 1.0),
    }


def reference(x_prompt, x_sample, rel_bias, norm_mix_w, norm_ffn_w, ffn_w_in, ffn_w_out, ev_w_in, ev_w_out,
              ssd_conv_w, ssd_conv_b, ssd_dt_bias, ssd_a_log, ssd_d, ssd_norm_w, rwkv_mu, rwkv_w0, rwkv_w2,
              rwkv_a0, rwkv_a2, rwkv_g2, rwkv_k_k, rwkv_k_a, rwkv_r_k, rwkv_ln_w, rwkv_ln_b, od_w_in, od_w_out,
              conv_dw_w, conv_dw_b, conv_ln_w, conv_ln_b, att_q_norm_w, att_k_norm_w, att_sink):
    weights = (rel_bias, norm_mix_w, norm_ffn_w, ffn_w_in, ffn_w_out, ev_w_in, ev_w_out, ssd_conv_w, ssd_conv_b,
               ssd_dt_bias, ssd_a_log, ssd_d, ssd_norm_w, rwkv_mu, rwkv_w0, rwkv_w2, rwkv_a0, rwkv_a2, rwkv_g2,
               rwkv_k_k, rwkv_k_a, rwkv_r_k, rwkv_ln_w, rwkv_ln_b, od_w_in, od_w_out, conv_dw_w, conv_dw_b,
               conv_ln_w, conv_ln_b, att_q_norm_w, att_k_norm_w, att_sink)
    y_prompt = _trunk(x_prompt, *weights)
    y_sample = _trunk(x_sample, *weights)
    return (y_prompt, y_sample)
```

```python
import functools
import math

import jax
import jax.numpy as jnp
import numpy as np
from jax import lax
from jax.experimental import pallas as pl
from jax.experimental.pallas import tpu as pltpu

F32 = jnp.float32
BF16 = jnp.bfloat16
HIGHEST = lax.Precision.HIGHEST

LANES = 128
SUBLANES = 8
VMEM_LIMIT_BYTES = 56 * 1024 * 1024

D_MODEL = 1024
HEAD_DIM = 64
D_FF = 2816
FF_CHUNK = 256
N_HEADS = 16
SSD_GROUPS = 2
SSD_STATE = 128
SSD_CONV = 5
SSD_XBC = D_MODEL + 2 * SSD_GROUPS * SSD_STATE
CHUNK = 128
RWKV_LN_EPS = 64e-5
CONV_WIDTH = 31
CONV_HALO = 16
ATT_BLOCK = 128
ATT_KV_HEADS = 4
ATT_GQ = 4
REL_BUCKETS = 32
REL_MAX_DIST = 128

EV_W = 6144
EV_RKV_W, EV_XBC_W, EV_LR_W, EV_DT_W, EV_Z_W = 3072, SSD_XBC, 256, LANES, D_MODEL
EV_XBC_BLK = 3072 // EV_XBC_W
EV_LR_BLK = 4608 // EV_LR_W
EV_DT_BLK = 4864 // EV_DT_W
EV_Z_BLK = 5120 // EV_Z_W


def _cparams(*sem):
    return pltpu.CompilerParams(dimension_semantics=sem, vmem_limit_bytes=VMEM_LIMIT_BYTES)


def _mm(a, b):
    return jnp.dot(a.astype(BF16), b.astype(BF16), preferred_element_type=F32)


def _mm_nt(a, b):
    return lax.dot_general(a.astype(BF16), b.astype(BF16), (((1,), (1,)), ((), ())),
                           preferred_element_type=F32)


def _mm_tn(a, b):
    return lax.dot_general(a.astype(BF16), b.astype(BF16), (((0,), (0,)), ((), ())),
                           preferred_element_type=F32)


def _rms(x, w, eps=1e-6):
    return x * lax.rsqrt(jnp.mean(x * x, axis=-1, keepdims=True) + eps) * w


def _norm_matmul_kernel(x_ref, nw_ref, w_ref, o_ref, h_ref):
    @pl.when(pl.program_id(1) == 0)
    def _():
        h_ref[...] = _rms(x_ref[...], nw_ref[...]).astype(BF16)

    o_ref[...] = jnp.dot(h_ref[...], w_ref[...], preferred_element_type=F32)


def _norm_matmul(x2, nw, w, tm, tn):
    n, d = x2.shape
    nout = w.shape[1]
    return pl.pallas_call(
        _norm_matmul_kernel,
        grid=(n // tm, nout // tn),
        in_specs=[pl.BlockSpec((tm, d), lambda i, j: (i, 0)),
                  pl.BlockSpec((1, d), lambda i, j: (0, 0)),
                  pl.BlockSpec((d, tn), lambda i, j: (0, j))],
        out_specs=pl.BlockSpec((tm, tn), lambda i, j: (i, j)),
        out_shape=jax.ShapeDtypeStruct((n, nout), F32),
        scratch_shapes=[pltpu.VMEM((tm, d), BF16)],
        compiler_params=_cparams("parallel", "arbitrary"),
        name="norm_matmul",
    )(x2, nw, w)


def _proj_res_kernel(x_ref, a_ref, b_ref, wa_ref, wb_ref, o_ref):
    o_ref[...] = x_ref[...] + _mm(a_ref[...], wa_ref[...]) + _mm(b_ref[...], wb_ref[...])


def _proj_res(x2, a2, b2, w, tm):
    n, d = x2.shape
    row = pl.BlockSpec((tm, d), lambda i: (i, 0))
    return pl.pallas_call(
        _proj_res_kernel,
        grid=(n // tm,),
        in_specs=[row, row, row,
                  pl.BlockSpec((d, d), lambda i: (0, 0)),
                  pl.BlockSpec((d, d), lambda i: (1, 0))],
        out_specs=row,
        out_shape=jax.ShapeDtypeStruct((n, d), F32),
        compiler_params=_cparams("parallel"),
        name="proj_res",
    )(x2, a2, b2, w, w)


def _ffn_kernel(x_ref, nw_ref, wi_ref, wo_ref, o_ref):
    x = x_ref[...]
    h = _rms(x, nw_ref[...]).astype(BF16)
    o_ref[...] = x
    for c in range(D_FF // FF_CHUNK):
        lo = c * FF_CHUNK
        g = jnp.dot(h, wi_ref[:, lo:lo + FF_CHUNK], preferred_element_type=F32)
        u = jnp.dot(h, wi_ref[:, D_FF + lo:D_FF + lo + FF_CHUNK], preferred_element_type=F32)
        act = (g * jax.nn.sigmoid(g) * u).astype(BF16)
        o_ref[...] += jnp.dot(act, wo_ref[lo:lo + FF_CHUNK, :], preferred_element_type=F32)


def _ffn(x2, nw, wi, wo, tm):
    n, d = x2.shape
    row = pl.BlockSpec((tm, d), lambda i: (i, 0))
    return pl.pallas_call(
        _ffn_kernel,
        grid=(n // tm,),
        in_specs=[row,
                  pl.BlockSpec((1, d), lambda i: (0, 0)),
                  pl.BlockSpec((d, 2 * D_FF), lambda i: (0, 0), pipeline_mode=pl.Buffered(1)),
                  pl.BlockSpec((D_FF, d), lambda i: (0, 0), pipeline_mode=pl.Buffered(1))],
        out_specs=row,
        out_shape=jax.ShapeDtypeStruct((n, d), F32),
        compiler_params=_cparams("parallel"),
        name="ffn",
    )(x2, nw, wi, wo)


def _shift_mix(p, prev_row, next_row, mu):
    tt = p.shape[0]
    rows = lax.broadcasted_iota(jnp.int32, p.shape, 0)
    prev = jnp.where(rows == 0, prev_row, pltpu.roll(p, 1, 0))
    nxt = jnp.where(rows == tt - 1, next_row, pltpu.roll(p, tt - 1, 0))
    return p + mu * (0.5 * (prev + nxt) - p)


def _rwkv_prep_kernel(rkv_ref, rkv_p_ref, rkv_n_ref, lr_ref, lr_p_ref, lr_n_ref,
                      mu_rkv_ref, mu_lr_ref, a0_ref, a2_ref, w0_ref, w2_ref, g2_ref,
                      rkvs_ref, a_ref, lw0_ref, lw1_ref, g_ref, *, nt):
    t = pl.program_id(1)
    keep_p = (t > 0).astype(F32)
    keep_n = (t < nt - 1).astype(F32)
    rkvs_ref[0] = _shift_mix(rkv_ref[0], rkv_p_ref[0, 7:8, :] * keep_p, rkv_n_ref[0, 0:1, :] * keep_n,
                             mu_rkv_ref[...])
    lr = _shift_mix(lr_ref[0], lr_p_ref[0, 7:8, :] * keep_p, lr_n_ref[0, 0:1, :] * keep_n, mu_lr_ref[...])
    a_ref[0] = jax.nn.sigmoid(a0_ref[...] + _mm(lr, a2_ref[...]))
    g_ref[0] = _mm(jax.nn.sigmoid(lr), g2_ref[...])
    th = jnp.tanh(lr)
    for d, out in enumerate((lw0_ref, lw1_ref)):
        wlog = -jax.nn.softplus(-(w0_ref[d] + _mm(th, w2_ref[d]))) - 0.5
        out[0] = -jnp.exp(wlog)


def _rwkv_prep(u3, mu_rkv, mu_lr, a0, a2p, w0, w2p, g2p, tt):
    bsz, t, _ = u3.shape
    nt = t // tt
    r8 = tt // SUBLANES
    last8 = t // SUBLANES - 1
    d = D_MODEL

    def main(w, blk):
        return pl.BlockSpec((1, tt, w), lambda b, i: (b, i, blk))

    def prev(w, blk):
        return pl.BlockSpec((1, SUBLANES, w), lambda b, i: (b, jnp.maximum(i * r8 - 1, 0), blk))

    def nxt(w, blk):
        return pl.BlockSpec((1, SUBLANES, w), lambda b, i: (b, jnp.minimum((i + 1) * r8, last8), blk))

    def const(shape):
        return pl.BlockSpec(shape, lambda b, i: (0,) * len(shape))

    out = pl.BlockSpec((1, tt, d), lambda b, i: (b, i, 0))
    sds = jax.ShapeDtypeStruct((bsz, t, d), F32)
    return pl.pallas_call(
        functools.partial(_rwkv_prep_kernel, nt=nt),
        grid=(bsz, nt),
        in_specs=[main(EV_RKV_W, 0), prev(EV_RKV_W, 0), nxt(EV_RKV_W, 0),
                  main(EV_LR_W, EV_LR_BLK), prev(EV_LR_W, EV_LR_BLK), nxt(EV_LR_W, EV_LR_BLK),
                  const((1, EV_RKV_W)), const((1, EV_LR_W)), const((1, d)), const((EV_LR_W, d)),
                  const((2, 1, d)), const((2, EV_LR_W, d)), const((EV_LR_W, d))],
        out_specs=[pl.BlockSpec((1, tt, EV_RKV_W), lambda b, i: (b, i, 0)), out, out, out, out],
        out_shape=[jax.ShapeDtypeStruct((bsz, t, EV_RKV_W), F32), sds, sds, sds, sds],
        compiler_params=_cparams("parallel", "parallel"),
        name="rwkv_prep",
    )(u3, u3, u3, u3, u3, u3, mu_rkv, mu_lr, a0, a2p, w0, w2p, g2p)


def _head_sum(x, lane_lo):
    s0 = jnp.sum(jnp.where(lane_lo, x, 0.0), axis=-1, keepdims=True)
    s1 = jnp.sum(jnp.where(lane_lo, 0.0, x), axis=-1, keepdims=True)
    return jnp.where(lane_lo, s0, s1)


def _same_block(ti, si, log2_size):
    return jnp.right_shift(ti, log2_size) == jnp.right_shift(si, log2_size)


def _unit_lower_inverse(a_ab, ti, si, eye):
    sh = 4
    a_bd = jnp.where(_same_block(ti, si, sh), a_ab, 0.0)
    t_inv = eye + a_bd
    x = a_bd
    for _ in range(sh - 1):
        x = _mm(x, x)
        t_inv = t_inv + _mm(t_inv, x)
    while (1 << sh) < a_ab.shape[0]:
        a_off = jnp.where(_same_block(ti, si, sh), 0.0, jnp.where(_same_block(ti, si, sh + 1), a_ab, 0.0))
        t_inv = t_inv + _mm(t_inv, _mm(a_off, t_inv))
        sh += 1
    return t_inv


def _rwkv_scan_kernel(*refs, rev, nc, final):
    if final:
        (r_ref, k_ref, v_ref, a_ref, lw_ref, kk_w_ref, ka_w_ref,
         yf_ref, g_ref, rk_w_ref, lnw_ref, lnb_ref, o_ref, s_ref) = refs
    else:
        r_ref, k_ref, v_ref, a_ref, lw_ref, kk_w_ref, ka_w_ref, o_ref, s_ref = refs
    L = CHUNK

    @pl.when(pl.program_id(2) == 0)
    def _():
        s_ref[...] = jnp.zeros_like(s_ref)

    r, k, v, a, lw = r_ref[0], k_ref[0], v_ref[0], a_ref[0], lw_ref[0]
    lane = lax.broadcasted_iota(jnp.int32, (L, LANES), 1)
    lane_lo = lane < HEAD_DIM
    ti = lax.broadcasted_iota(jnp.int32, (L, L), 0)
    si = lax.broadcasted_iota(jnp.int32, (L, L), 1)
    if rev:
        tri = (si >= ti)
        strict = si > ti
    else:
        tri = (si <= ti)
        strict = si < ti
    eye = (si == ti).astype(F32)

    kk = k * kk_w_ref[...]
    kk = kk / jnp.maximum(jnp.sqrt(_head_sum(kk * kk, lane_lo)), 1e-12)
    kmod = k * (1.0 + (a - 1.0) * ka_w_ref[...])
    ia = -kk
    ib = kk * a

    cum = jnp.dot(tri.astype(F32), lw, precision=HIGHEST, preferred_element_type=F32)
    tot = cum[0:1, :] if rev else cum[L - 1:L, :]
    p_inv = jnp.exp(-cum)
    p_end = jnp.exp(tot - cum)
    a_t = ia * jnp.exp(cum - lw)
    b_t = ib * p_inv
    k_t = kmod * p_inv
    r_t = r * jnp.exp(cum)
    b_h = ib * p_end
    k_h = kmod * p_end

    rhs = jnp.concatenate([b_t, k_t], axis=0).astype(BF16)
    t_heads, m_heads, x_heads = [], [], []
    for h in range(2):
        mh = lane_lo if h == 0 else jnp.logical_not(lane_lo)
        a_m = jnp.where(mh, a_t, 0.0)
        lhs = jnp.concatenate([a_m, jnp.where(mh, r_t, 0.0)], axis=0)
        gram = _mm_nt(lhs, rhs)
        a_ab = jnp.where(strict, gram[:L, :L], 0.0)
        a_ak = jnp.where(strict, gram[:L, L:], 0.0)
        m_rb = jnp.where(tri, gram[L:, :L], 0.0)
        m_rk = jnp.where(tri, gram[L:, L:], 0.0)
        t_heads.append(_unit_lower_inverse(a_ab, ti, si, eye))
        m_heads.append((m_rb, m_rk))
        x_heads.append(jnp.concatenate([a_m, _mm(a_ak, jnp.where(mh, v, 0.0))], axis=1))

    w12 = _mm(jnp.concatenate(t_heads, axis=1), jnp.concatenate(x_heads, axis=0))
    s0 = s_ref[...]
    u = _mm_nt(w12[:, :LANES], s0) + w12[:, LANES:]
    u_lo = jnp.where(lane_lo, u, 0.0)
    v_lo = jnp.where(lane_lo, v, 0.0)
    uv = jnp.concatenate([u_lo, u - u_lo, v_lo, v - v_lo], axis=0)
    m_cat = jnp.concatenate([m_heads[0][0], m_heads[1][0], m_heads[0][1], m_heads[1][1]], axis=1)
    y = _mm_nt(r_t, s0) + _mm(m_cat, uv)
    s_new = s0 * jnp.exp(tot) + _mm_tn(jnp.concatenate([u, v], axis=0), jnp.concatenate([b_h, k_h], axis=0))
    s_ref[...] = jnp.where(_same_block(ti, si, 6), s_new, 0.0)

    if not final:
        o_ref[0] = y
    else:
        y = y + yf_ref[0]
        inv_n = 1.0 / HEAD_DIM
        yc = y - _head_sum(y, lane_lo) * inv_n
        yn = yc * lax.rsqrt(_head_sum(yc * yc, lane_lo) * inv_n + RWKV_LN_EPS)
        bonus = _head_sum(r * kmod * rk_w_ref[...], lane_lo)
        o_ref[0] = (yn * lnw_ref[...] + lnb_ref[...] + bonus * v) * g_ref[0]


def _rwkv_scan(rkvs, a, lw, kk_w, ka_w, rev, final_args=None):
    bsz, t, _ = rkvs.shape
    nc = t // CHUNK
    npair = D_MODEL // LANES
    final = final_args is not None

    def tmap(off):
        if rev:
            return lambda b, p, c: (b, nc - 1 - c, off + p)
        return lambda b, p, c: (b, c, off + p)

    blk = lambda off: pl.BlockSpec((1, CHUNK, LANES), tmap(off))
    par = pl.BlockSpec((1, LANES), lambda b, p, c: (0, p))
    in_specs = [blk(0), blk(npair), blk(2 * npair), blk(0), blk(0), par, par]
    args = [rkvs, rkvs, rkvs, a, lw, kk_w, ka_w]
    if final:
        yf, g, rk_w, ln_w, ln_b = final_args
        in_specs += [blk(0), blk(0), par, par, par]
        args += [yf, g, rk_w, ln_w, ln_b]
    return pl.pallas_call(
        functools.partial(_rwkv_scan_kernel, rev=rev, nc=nc, final=final),
        grid=(bsz, npair, nc),
        in_specs=in_specs,
        out_specs=blk(0),
        out_shape=jax.ShapeDtypeStruct((bsz, t, D_MODEL), F32),
        scratch_shapes=[pltpu.VMEM((LANES, LANES), F32)],
        compiler_params=_cparams("parallel", "parallel", "arbitrary"),
        name="rwkv_scan_bwd" if rev else "rwkv_scan_fwd",
    )(*args)


def _ssd_kernel(*refs, rev, nc, final):
    if final:
        (xbc_ref, xp_ref, xn_ref, dt_ref, cw_ref, cb_ref, dtb_ref, alog_ref,
         z_ref, yf_ref, dsk_ref, nw_ref, o_ref, h_ref, pad_ref, y_ref) = refs
    else:
        (xbc_ref, xp_ref, xn_ref, dt_ref, cw_ref, cb_ref, dtb_ref, alog_ref,
         o_ref, h_ref, pad_ref, y_ref) = refs
    L = CHUNK
    c = pl.program_id(1)
    cc = nc - 1 - c if rev else c

    @pl.when(c == 0)
    def _():
        h_ref[...] = jnp.zeros_like(h_ref)

    pad_ref[0:SUBLANES, :] = xp_ref[0] * (cc > 0).astype(F32)
    pad_ref[SUBLANES:SUBLANES + L, :] = xbc_ref[0]
    pad_ref[SUBLANES + L:2 * SUBLANES + L, :] = xn_ref[0] * (cc < nc - 1).astype(F32)
    acc = cb_ref[...] + cw_ref[0:1, :] * pad_ref[SUBLANES - 2:SUBLANES - 2 + L, :]
    for j in range(1, SSD_CONV):
        lo = SUBLANES - 2 + j
        acc = acc + cw_ref[j:j + 1, :] * pad_ref[lo:lo + L, :]
    xa = acc * jax.nn.sigmoid(acc)
    xs = xa[:, :D_MODEL]
    n_bc = SSD_GROUPS * SSD_STATE
    b_all = xa[:, D_MODEL:D_MODEL + n_bc]
    c_all = xa[:, D_MODEL + n_bc:]

    dtv = jax.nn.softplus(dt_ref[0] + dtb_ref[...])
    da = dtv * -jnp.exp(alog_ref[...])
    ti = lax.broadcasted_iota(jnp.int32, (L, L), 0)
    si = lax.broadcasted_iota(jnp.int32, (L, L), 1)
    tri = (si >= ti) if rev else (si <= ti)
    cs = jnp.dot(tri.astype(F32), da, precision=HIGHEST, preferred_element_type=F32)
    cs_t = cs.T
    edge = 0 if rev else L - 1
    hg = N_HEADS // SSD_GROUPS
    for g in range(SSD_GROUPS):
        bm = b_all[:, g * SSD_STATE:(g + 1) * SSD_STATE]
        cm = c_all[:, g * SSD_STATE:(g + 1) * SSD_STATE]
        cb = _mm_nt(cm, bm)
        bm_t = bm.T
        for hh in range(g * hg, (g + 1) * hg):
            col_i = (N_HEADS if rev else 0) + hh
            col = cs[:, col_i:col_i + 1]
            row = cs_t[col_i:col_i + 1, :]
            decay = jnp.exp(jnp.where(tri, col - row, -jnp.inf))
            xdt = xs[:, hh * HEAD_DIM:(hh + 1) * HEAD_DIM] * dtv[:, col_i:col_i + 1]
            h0 = h_ref[hh]
            y = _mm(cb * decay, xdt) + _mm(cm, h0) * jnp.exp(col)
            tot = cs[edge:edge + 1, col_i:col_i + 1]
            h_ref[hh] = h0 * jnp.exp(tot) + _mm(bm_t, xdt * jnp.exp(tot - col))
            y_ref[:, hh * HEAD_DIM:(hh + 1) * HEAD_DIM] = y

    y = y_ref[...]
    if not final:
        o_ref[0] = y
    else:
        z = z_ref[0]
        y = (y + yf_ref[0] + dsk_ref[...] * xs) * (z * jax.nn.sigmoid(z))
        gw = D_MODEL // SSD_GROUPS
        parts = []
        for g in range(SSD_GROUPS):
            yg = y[:, g * gw:(g + 1) * gw]
            parts.append(yg * lax.rsqrt(jnp.mean(yg * yg, axis=-1, keepdims=True) + 1e-6))
        o_ref[0] = jnp.concatenate(parts, axis=1) * nw_ref[...]


def _ssd(u3, conv_w, conv_b, dt_bias, a_log, rev, final_args=None):
    bsz, t, _ = u3.shape
    nc = t // CHUNK
    r8 = CHUNK // SUBLANES
    last8 = t // SUBLANES - 1
    final = final_args is not None
    cmap = (lambda c: nc - 1 - c) if rev else (lambda c: c)

    def const(shape):
        return pl.BlockSpec(shape, lambda b, c: (0,) * len(shape))

    in_specs = [
        pl.BlockSpec((1, CHUNK, EV_XBC_W), lambda b, c: (b, cmap(c), EV_XBC_BLK)),
        pl.BlockSpec((1, SUBLANES, EV_XBC_W), lambda b, c: (b, jnp.maximum(cmap(c) * r8 - 1, 0), EV_XBC_BLK)),
        pl.BlockSpec((1, SUBLANES, EV_XBC_W),
                     lambda b, c: (b, jnp.minimum((cmap(c) + 1) * r8, last8), EV_XBC_BLK)),
        pl.BlockSpec((1, CHUNK, EV_DT_W), lambda b, c: (b, cmap(c), EV_DT_BLK)),
        const((SUBLANES, EV_XBC_W)), const((1, EV_XBC_W)), const((1, LANES)), const((1, LANES)),
    ]
    args = [u3, u3, u3, u3, conv_w, conv_b, dt_bias, a_log]
    if final:
        yf, d_skip, norm_w = final_args
        in_specs += [pl.BlockSpec((1, CHUNK, EV_Z_W), lambda b, c: (b, cmap(c), EV_Z_BLK)),
                     pl.BlockSpec((1, CHUNK, D_MODEL), lambda b, c: (b, cmap(c), 0)),
                     const((1, D_MODEL)), const((1, D_MODEL))]
        args += [u3, yf, d_skip, norm_w]
    return pl.pallas_call(
        functools.partial(_ssd_kernel, rev=rev, nc=nc, final=final),
        grid=(bsz, nc),
        in_specs=in_specs,
        out_specs=pl.BlockSpec((1, CHUNK, D_MODEL), lambda b, c: (b, cmap(c), 0)),
        out_shape=jax.ShapeDtypeStruct((bsz, t, D_MODEL), F32),
        scratch_shapes=[pltpu.VMEM((N_HEADS, SSD_STATE, HEAD_DIM), F32),
                        pltpu.VMEM((CHUNK + 2 * SUBLANES, EV_XBC_W), F32),
                        pltpu.VMEM((CHUNK, D_MODEL), F32)],
        compiler_params=_cparams("parallel", "arbitrary"),
        name="ssd_bwd" if rev else "ssd_fwd",
    )(*args)


def _conv_module_kernel(val_ref, gate_ref, vp_ref, gp_ref, vn_ref, gn_ref,
                        dw_ref, db_ref, lnw_ref, lnb_ref, o_ref, pad_ref, *, nt):
    t = pl.program_id(1)
    tt = val_ref.shape[1]
    h = CONV_HALO
    pad_ref[0:h, :] = vp_ref[0] * jax.nn.sigmoid(gp_ref[0]) * (t > 0).astype(F32)
    pad_ref[h:h + tt, :] = val_ref[0] * jax.nn.sigmoid(gate_ref[0])
    pad_ref[h + tt:2 * h + tt, :] = vn_ref[0] * jax.nn.sigmoid(gn_ref[0]) * (t < nt - 1).astype(F32)
    base = h - CONV_WIDTH // 2
    acc = db_ref[...] + dw_ref[0:1, :] * pad_ref[base:base + tt, :]
    for j in range(1, CONV_WIDTH):
        acc = acc + dw_ref[j:j + 1, :] * pad_ref[base + j:base + j + tt, :]
    xc = acc - jnp.mean(acc, axis=-1, keepdims=True)
    xn = xc * lax.rsqrt(jnp.mean(xc * xc, axis=-1, keepdims=True) + 1e-5)
    yv = xn * lnw_ref[...] + lnb_ref[...]
    o_ref[0] = yv * jax.nn.sigmoid(yv)


def _conv_module(u3, dw_w, dw_b, ln_w, ln_b, tt):
    bsz, t, _ = u3.shape
    nt = t // tt
    rh = tt // CONV_HALO
    lasth = t // CONV_HALO - 1
    d = D_MODEL

    def const(shape):
        return pl.BlockSpec(shape, lambda b, i: (0,) * len(shape))

    main = lambda blk: pl.BlockSpec((1, tt, d), lambda b, i: (b, i, blk))
    prev = lambda blk: pl.BlockSpec((1, CONV_HALO, d), lambda b, i: (b, jnp.maximum(i * rh - 1, 0), blk))
    nxt = lambda blk: pl.BlockSpec((1, CONV_HALO, d), lambda b, i: (b, jnp.minimum((i + 1) * rh, lasth), blk))
    return pl.pallas_call(
        functools.partial(_conv_module_kernel, nt=nt),
        grid=(bsz, nt),
        in_specs=[main(0), main(1), prev(0), prev(1), nxt(0), nxt(1),
                  const((4 * SUBLANES, d)), const((1, d)), const((1, d)), const((1, d))],
        out_specs=pl.BlockSpec((1, tt, d), lambda b, i: (b, i, 0)),
        out_shape=jax.ShapeDtypeStruct((bsz, t, d), F32),
        scratch_shapes=[pltpu.VMEM((tt + 2 * CONV_HALO, d), F32)],
        compiler_params=_cparams("parallel", "parallel"),
        name="conv_module",
    )(u3, u3, u3, u3, u3, u3, dw_w, dw_b, ln_w, ln_b)


def _attention_kernel(q_ref, k0_ref, k1_ref, k2_ref, v0_ref, v1_ref, v2_ref,
                      bias_ref, qw_ref, kw_ref, sink_ref, o_ref, *, nblk):
    i = pl.program_id(1)
    blk = ATT_BLOCK
    col = lax.broadcasted_iota(jnp.int32, (blk, 3 * blk), 1)
    oob = ((i == 0) & (col < blk)) | ((i == nblk - 1) & (col >= 2 * blk))
    scale = HEAD_DIM ** -0.5
    k_all = jnp.concatenate([k0_ref[0], k1_ref[0], k2_ref[0]], axis=0)
    v_all = jnp.concatenate([v0_ref[0], v1_ref[0], v2_ref[0]], axis=0)
    q_all = q_ref[0]
    for kv in range(ATT_KV_HEADS):
        ks = _rms(k_all[:, kv * HEAD_DIM:(kv + 1) * HEAD_DIM], kw_ref[...])
        vs = v_all[:, kv * HEAD_DIM:(kv + 1) * HEAD_DIM]
        for gq in range(ATT_GQ):
            hq = kv * ATT_GQ + gq
            qs = _rms(q_all[:, hq * HEAD_DIM:(hq + 1) * HEAD_DIM], qw_ref[...])
            logits = _mm_nt(qs, ks) * scale + bias_ref[hq]
            logits = jnp.where(oob, -jnp.inf, logits)
            sink = sink_ref[hq]
            m = jnp.maximum(jnp.max(logits, axis=-1, keepdims=True), sink)
            p = jnp.exp(logits - m)
            denom = jnp.sum(p, axis=-1, keepdims=True) + jnp.exp(sink - m)
            o_ref[0, :, hq * HEAD_DIM:(hq + 1) * HEAD_DIM] = _mm(p, vs) / denom


def _attention(u3, bias, q_norm_w, k_norm_w, sink):
    bsz, t, _ = u3.shape
    nblk = t // ATT_BLOCK
    d = D_MODEL
    kvw = ATT_KV_HEADS * HEAD_DIM
    kblk = (2 * D_MODEL + d) // kvw
    vblk = kblk + 1

    def kv_spec(blk, off):
        return pl.BlockSpec((1, ATT_BLOCK, kvw), lambda b, i: (b, jnp.clip(i + off, 0, nblk - 1), blk))

    def const(shape):
        return pl.BlockSpec(shape, lambda b, i: (0,) * len(shape))

    return pl.pallas_call(
        functools.partial(_attention_kernel, nblk=nblk),
        grid=(bsz, nblk),
        in_specs=[pl.BlockSpec((1, ATT_BLOCK, d), lambda b, i: (b, i, 2)),
                  kv_spec(kblk, -1), kv_spec(kblk, 0), kv_spec(kblk, 1),
                  kv_spec(vblk, -1), kv_spec(vblk, 0), kv_spec(vblk, 1),
                  const((N_HEADS, ATT_BLOCK, 3 * ATT_BLOCK)), const((1, HEAD_DIM)), const((1, HEAD_DIM)),
                  pl.BlockSpec(memory_space=pltpu.SMEM)],
        out_specs=pl.BlockSpec((1, ATT_BLOCK, d), lambda b, i: (b, i, 0)),
        out_shape=jax.ShapeDtypeStruct((bsz, t, d), F32),
        compiler_params=_cparams("parallel", "parallel"),
        name="attention",
    )(u3, u3, u3, u3, u3, u3, u3, bias, q_norm_w, k_norm_w, sink)


def _t5_bucket(rel):
    nb = REL_BUCKETS // 2
    max_exact = nb // 2
    n = jnp.abs(rel)
    nf = jnp.maximum(n, 1).astype(jnp.float32)
    large = max_exact + (jnp.log(nf / max_exact) / math.log(REL_MAX_DIST / max_exact)
                         * (nb - max_exact)).astype(jnp.int32)
    large = jnp.minimum(large, nb - 1)
    return (rel > 0).astype(jnp.int32) * nb + jnp.where(n < max_exact, n, large)


def _attention_bias(rel_bias):
    rel = jnp.arange(3 * ATT_BLOCK)[None, :] - ATT_BLOCK - jnp.arange(ATT_BLOCK)[:, None]
    bias = rel_bias.astype(F32)[_t5_bucket(rel)].transpose(2, 0, 1)
    return jnp.where((jnp.abs(rel) <= ATT_BLOCK)[None], bias, -jnp.inf)


def _pad_cols(w, width):
    return jnp.pad(w, ((0, 0), (0, width - w.shape[1])))


def _even_in_weight(w_in):
    z = w_in[:, :D_MODEL]
    xbc = w_in[:, D_MODEL:D_MODEL + SSD_XBC]
    dt = w_in[:, D_MODEL + SSD_XBC:D_MODEL + SSD_XBC + 2 * N_HEADS]
    p = w_in[:, D_MODEL + SSD_XBC + 2 * N_HEADS:]
    w = jnp.concatenate([p[:, :EV_RKV_W], xbc, p[:, EV_RKV_W:], _pad_cols(dt, 2 * LANES), z], axis=1)
    assert w.shape[1] == EV_W
    return w.astype(BF16)


def _row(v):
    return v.reshape(1, -1).astype(F32)


def _even_params(e, ev_w_in, ev_w_out, ssd_conv_w, ssd_conv_b, ssd_dt_bias, ssd_a_log, ssd_d, ssd_norm_w,
                 rwkv_mu, rwkv_w0, rwkv_w2, rwkv_a0, rwkv_a2, rwkv_g2, rwkv_k_k, rwkv_k_a, rwkv_r_k,
                 rwkv_ln_w, rwkv_ln_b):
    zeros = lambda n: jnp.zeros((n, D_MODEL), F32)
    lr_pad = lambda w, lo: jnp.concatenate([zeros(lo), w.astype(F32), zeros(EV_LR_W - lo - w.shape[0])], 0).astype(BF16)
    return dict(
        w_in=_even_in_weight(ev_w_in[e]),
        w_out=ev_w_out[e].astype(BF16),
        conv_w=jnp.pad(ssd_conv_w[e].astype(F32), ((0, SUBLANES - SSD_CONV), (0, 0))),
        conv_b=_row(ssd_conv_b[e]),
        dt_bias=_pad_cols(_row(ssd_dt_bias[e]), LANES),
        a_log=_pad_cols(_row(ssd_a_log[e]), LANES),
        d_skip=_row(jnp.repeat(ssd_d[e], HEAD_DIM)),
        ssd_norm_w=_row(ssd_norm_w[e]),
        mu_rkv=_row(rwkv_mu[e, :EV_RKV_W]),
        mu_lr=_row(rwkv_mu[e, EV_RKV_W:]),
        a0=_row(rwkv_a0[e]),
        w0=rwkv_w0[e].reshape(2, 1, D_MODEL).astype(F32),
        w2=jnp.stack([lr_pad(rwkv_w2[e, d], 0) for d in range(2)]),
        a2=lr_pad(rwkv_a2[e], 64),
        g2=lr_pad(rwkv_g2[e], 128),
        k_k=_row(rwkv_k_k[e]), k_a=_row(rwkv_k_a[e]), r_k=_row(rwkv_r_k[e]),
        ln_w=_row(rwkv_ln_w[e]), ln_b=_row(rwkv_ln_b[e]),
    )


def _even_layer(x3, norm_w, p, tm):
    bsz, t, d = x3.shape
    x2 = x3.reshape(bsz * t, d)
    u3 = _norm_matmul(x2, norm_w, p["w_in"], tm, 1024).reshape(bsz, t, EV_W)
    ssd_args = (u3, p["conv_w"], p["conv_b"], p["dt_bias"], p["a_log"])
    ya_f = _ssd(*ssd_args, rev=False)
    ya = _ssd(*ssd_args, rev=True, final_args=(ya_f, p["d_skip"], p["ssd_norm_w"]))
    rkvs, a, lw0, lw1, g = _rwkv_prep(u3, p["mu_rkv"], p["mu_lr"], p["a0"], p["a2"], p["w0"], p["w2"], p["g2"],
                                      min(t, 256))
    yb_f = _rwkv_scan(rkvs, a, lw0, p["k_k"], p["k_a"], rev=False)
    yb = _rwkv_scan(rkvs, a, lw1, p["k_k"], p["k_a"], rev=True,
                    final_args=(yb_f, g, p["r_k"], p["ln_w"], p["ln_b"]))
    out = _proj_res(x2, ya.reshape(bsz * t, d), yb.reshape(bsz * t, d), p["w_out"], tm)
    return out.reshape(bsz, t, d)


def _odd_layer(x3, norm_w, p, bias, tm):
    bsz, t, d = x3.shape
    x2 = x3.reshape(bsz * t, d)
    u3 = _norm_matmul(x2, norm_w, p["w_in"], tm, 512).reshape(bsz, t, -1)
    yc = _conv_module(u3, p["dw_w"], p["dw_b"], p["ln_w"], p["ln_b"], min(t, 256))
    yd = _attention(u3, bias, p["q_w"], p["k_w"], p["sink"])
    out = _proj_res(x2, yc.reshape(bsz * t, d), yd.reshape(bsz * t, d), p["w_out"], tm)
    return out.reshape(bsz, t, d)


def _trunk(x3, layers, bias):
    bsz, t, d = x3.shape
    tm = min(bsz * t, 512)
    for kind, norm_mix, p, norm_ffn, ffn_wi, ffn_wo in layers:
        if kind == "even":
            x3 = _even_layer(x3, norm_mix, p, tm)
        else:
            x3 = _odd_layer(x3, norm_mix, p, bias, tm)
        x3 = _ffn(x3.reshape(bsz * t, d), norm_ffn, ffn_wi, ffn_wo, tm).reshape(bsz, t, d)
    return x3


def kernel(x_prompt, x_sample, rel_bias, norm_mix_w, norm_ffn_w, ffn_w_in, ffn_w_out, ev_w_in, ev_w_out, ssd_conv_w, ssd_conv_b, ssd_dt_bias, ssd_a_log, ssd_d, ssd_norm_w, rwkv_mu, rwkv_w0, rwkv_w2, rwkv_a0, rwkv_a2, rwkv_g2, rwkv_k_k, rwkv_k_a, rwkv_r_k, rwkv_ln_w, rwkv_ln_b, od_w_in, od_w_out, conv_dw_w, conv_dw_b, conv_ln_w, conv_ln_b, att_q_norm_w, att_k_norm_w, att_sink):
    depth = norm_mix_w.shape[0]
    bias = _attention_bias(rel_bias)
    layers = []
    for layer in range(depth):
        i = layer // 2
        if layer % 2 == 0:
            kind = "even"
            p = _even_params(i, ev_w_in, ev_w_out, ssd_conv_w, ssd_conv_b, ssd_dt_bias, ssd_a_log, ssd_d,
                             ssd_norm_w, rwkv_mu, rwkv_w0, rwkv_w2, rwkv_a0, rwkv_a2, rwkv_g2, rwkv_k_k,
                             rwkv_k_a, rwkv_r_k, rwkv_ln_w, rwkv_ln_b)
        else:
            kind = "odd"
            p = dict(
                w_in=od_w_in[i].astype(BF16), w_out=od_w_out[i].astype(BF16),
                dw_w=jnp.pad(conv_dw_w[i].astype(F32), ((0, 4 * SUBLANES - CONV_WIDTH), (0, 0))),
                dw_b=_row(conv_dw_b[i]), ln_w=_row(conv_ln_w[i]), ln_b=_row(conv_ln_b[i]),
                q_w=_row(att_q_norm_w[i]), k_w=_row(att_k_norm_w[i]), sink=att_sink[i].astype(F32),
            )
        layers.append((kind, _row(norm_mix_w[layer]), p, _row(norm_ffn_w[layer]),
                       ffn_w_in[layer].astype(BF16), ffn_w_out[layer].astype(BF16)))
    return (_trunk(x_prompt, layers, bias), _trunk(x_sample, layers, bias))
```

```python
import functools
import math

import jax
import jax.numpy as jnp
import numpy as np
from jax import lax
from jax.experimental import pallas as pl
from jax.experimental.pallas import tpu as pltpu

F32 = jnp.float32
BF16 = jnp.bfloat16
HIGHEST = lax.Precision.HIGHEST

LANES = 128
SUBLANES = 8
VMEM_LIMIT_BYTES = 56 * 1024 * 1024

D_MODEL = 1024
HEAD_DIM = 64
D_FF = 2816
FF_CHUNK = 256
N_HEADS = 16
SSD_GROUPS = 2
SSD_STATE = 128
SSD_CONV = 5
SSD_XBC = D_MODEL + 2 * SSD_GROUPS * SSD_STATE
CHUNK = 128
RWKV_LN_EPS = 64e-5
RWKV_PAIRS_PER_STEP = 8
CONV_WIDTH = 31
CONV_HALO = 16
ATT_BLOCK = 128
ATT_KV_HEADS = 4
ATT_GQ = 4
REL_BUCKETS = 32
REL_MAX_DIST = 128

EV_W = 6144
EV_RKV_W, EV_XBC_W, EV_LR_W, EV_DT_W, EV_Z_W = 3072, SSD_XBC, 256, LANES, D_MODEL
EV_XBC_BLK = 3072 // EV_XBC_W
EV_LR_BLK = 4608 // EV_LR_W
EV_DT_BLK = 4864 // EV_DT_W
EV_Z_BLK = 5120 // EV_Z_W


def _cparams(*sem):
    return pltpu.CompilerParams(dimension_semantics=sem, vmem_limit_bytes=VMEM_LIMIT_BYTES)


def _mm(a, b):
    return jnp.dot(a.astype(BF16), b.astype(BF16), preferred_element_type=F32)


def _mm_nt(a, b):
    return lax.dot_general(a.astype(BF16), b.astype(BF16), (((1,), (1,)), ((), ())),
                           preferred_element_type=F32)


def _mm_tn(a, b):
    return lax.dot_general(a.astype(BF16), b.astype(BF16), (((0,), (0,)), ((), ())),
                           preferred_element_type=F32)


def _rms(x, w, eps=1e-6):
    return x * lax.rsqrt(jnp.mean(x * x, axis=-1, keepdims=True) + eps) * w


def _norm_matmul_kernel(x_ref, nw_ref, w_ref, o_ref, h_ref):
    @pl.when(pl.program_id(1) == 0)
    def _():
        h_ref[...] = _rms(x_ref[...], nw_ref[...]).astype(BF16)

    o_ref[...] = jnp.dot(h_ref[...], w_ref[...], preferred_element_type=F32)


def _norm_matmul(x2, nw, w, tm, tn):
    n, d = x2.shape
    nout = w.shape[1]
    return pl.pallas_call(
        _norm_matmul_kernel,
        grid=(n // tm, nout // tn),
        in_specs=[pl.BlockSpec((tm, d), lambda i, j: (i, 0)),
                  pl.BlockSpec((1, d), lambda i, j: (0, 0)),
                  pl.BlockSpec((d, tn), lambda i, j: (0, j))],
        out_specs=pl.BlockSpec((tm, tn), lambda i, j: (i, j)),
        out_shape=jax.ShapeDtypeStruct((n, nout), F32),
        scratch_shapes=[pltpu.VMEM((tm, d), BF16)],
        compiler_params=_cparams("parallel", "arbitrary"),
        name="norm_matmul",
    )(x2, nw, w)


def _proj_res_kernel(x_ref, a_ref, b_ref, wa_ref, wb_ref, o_ref):
    o_ref[...] = x_ref[...] + _mm(a_ref[...], wa_ref[...]) + _mm(b_ref[...], wb_ref[...])


def _proj_res(x2, a2, b2, w, tm):
    n, d = x2.shape
    row = pl.BlockSpec((tm, d), lambda i: (i, 0))
    return pl.pallas_call(
        _proj_res_kernel,
        grid=(n // tm,),
        in_specs=[row, row, row,
                  pl.BlockSpec((d, d), lambda i: (0, 0)),
                  pl.BlockSpec((d, d), lambda i: (1, 0))],
        out_specs=row,
        out_shape=jax.ShapeDtypeStruct((n, d), F32),
        compiler_params=_cparams("parallel"),
        name="proj_res",
    )(x2, a2, b2, w, w)


def _ffn_kernel(x_ref, nw_ref, wi_ref, wo_ref, o_ref):
    x = x_ref[...]
    h = _rms(x, nw_ref[...]).astype(BF16)
    o_ref[...] = x
    for c in range(D_FF // FF_CHUNK):
        lo = c * FF_CHUNK
        g = jnp.dot(h, wi_ref[:, lo:lo + FF_CHUNK], preferred_element_type=F32)
        u = jnp.dot(h, wi_ref[:, D_FF + lo:D_FF + lo + FF_CHUNK], preferred_element_type=F32)
        act = (g * jax.nn.sigmoid(g) * u).astype(BF16)
        o_ref[...] += jnp.dot(act, wo_ref[lo:lo + FF_CHUNK, :], preferred_element_type=F32)


def _ffn(x2, nw, wi, wo, tm):
    n, d = x2.shape
    row = pl.BlockSpec((tm, d), lambda i: (i, 0))
    return pl.pallas_call(
        _ffn_kernel,
        grid=(n // tm,),
        in_specs=[row,
                  pl.BlockSpec((1, d), lambda i: (0, 0)),
                  pl.BlockSpec((d, 2 * D_FF), lambda i: (0, 0), pipeline_mode=pl.Buffered(1)),
                  pl.BlockSpec((D_FF, d), lambda i: (0, 0), pipeline_mode=pl.Buffered(1))],
        out_specs=row,
        out_shape=jax.ShapeDtypeStruct((n, d), F32),
        compiler_params=_cparams("parallel"),
        name="ffn",
    )(x2, nw, wi, wo)


def _shift_mix(p, prev_row, next_row, mu):
    tt = p.shape[0]
    rows = lax.broadcasted_iota(jnp.int32, p.shape, 0)
    prev = jnp.where(rows == 0, prev_row, pltpu.roll(p, 1, 0))
    nxt = jnp.where(rows == tt - 1, next_row, pltpu.roll(p, tt - 1, 0))
    return p + mu * (0.5 * (prev + nxt) - p)


def _rwkv_prep_kernel(rkv_ref, rkv_p_ref, rkv_n_ref, lr_ref, lr_p_ref, lr_n_ref,
                      mu_rkv_ref, mu_lr_ref, a0_ref, a2_ref, w0_ref, w2_ref, g2_ref,
                      rkvs_ref, a_ref, lw0_ref, lw1_ref, g_ref, *, nt):
    t = pl.program_id(1)
    keep_p = (t > 0).astype(F32)
    keep_n = (t < nt - 1).astype(F32)
    rkvs_ref[0] = _shift_mix(rkv_ref[0], rkv_p_ref[0, 7:8, :] * keep_p, rkv_n_ref[0, 0:1, :] * keep_n,
                             mu_rkv_ref[...])
    lr = _shift_mix(lr_ref[0], lr_p_ref[0, 7:8, :] * keep_p, lr_n_ref[0, 0:1, :] * keep_n, mu_lr_ref[...])
    a_ref[0] = jax.nn.sigmoid(a0_ref[...] + _mm(lr, a2_ref[...]))
    g_ref[0] = _mm(jax.nn.sigmoid(lr), g2_ref[...])
    th = jnp.tanh(lr)
    for d, out in enumerate((lw0_ref, lw1_ref)):
        wlog = -jax.nn.softplus(-(w0_ref[d] + _mm(th, w2_ref[d]))) - 0.5
        out[0] = -jnp.exp(wlog)


def _rwkv_prep(u3, mu_rkv, mu_lr, a0, a2p, w0, w2p, g2p, tt):
    bsz, t, _ = u3.shape
    nt = t // tt
    r8 = tt // SUBLANES
    last8 = t // SUBLANES - 1
    d = D_MODEL

    def main(w, blk):
        return pl.BlockSpec((1, tt, w), lambda b, i: (b, i, blk))

    def prev(w, blk):
        return pl.BlockSpec((1, SUBLANES, w), lambda b, i: (b, jnp.maximum(i * r8 - 1, 0), blk))

    def nxt(w, blk):
        return pl.BlockSpec((1, SUBLANES, w), lambda b, i: (b, jnp.minimum((i + 1) * r8, last8), blk))

    def const(shape):
        return pl.BlockSpec(shape, lambda b, i: (0,) * len(shape))

    out = pl.BlockSpec((1, tt, d), lambda b, i: (b, i, 0))
    sds = jax.ShapeDtypeStruct((bsz, t, d), F32)
    return pl.pallas_call(
        functools.partial(_rwkv_prep_kernel, nt=nt),
        grid=(bsz, nt),
        in_specs=[main(EV_RKV_W, 0), prev(EV_RKV_W, 0), nxt(EV_RKV_W, 0),
                  main(EV_LR_W, EV_LR_BLK), prev(EV_LR_W, EV_LR_BLK), nxt(EV_LR_W, EV_LR_BLK),
                  const((1, EV_RKV_W)), const((1, EV_LR_W)), const((1, d)), const((EV_LR_W, d)),
                  const((2, 1, d)), const((2, EV_LR_W, d)), const((EV_LR_W, d))],
        out_specs=[pl.BlockSpec((1, tt, EV_RKV_W), lambda b, i: (b, i, 0)), out, out, out, out],
        out_shape=[jax.ShapeDtypeStruct((bsz, t, EV_RKV_W), F32), sds, sds, sds, sds],
        compiler_params=_cparams("parallel", "parallel"),
        name="rwkv_prep",
    )(u3, u3, u3, u3, u3, u3, mu_rkv, mu_lr, a0, a2p, w0, w2p, g2p)


def _head_sum(x, lane_lo):
    s0 = jnp.sum(jnp.where(lane_lo, x, 0.0), axis=-1, keepdims=True)
    s1 = jnp.sum(jnp.where(lane_lo, 0.0, x), axis=-1, keepdims=True)
    return jnp.where(lane_lo, s0, s1)


def _same_block(ti, si, log2_size):
    return jnp.right_shift(ti, log2_size) == jnp.right_shift(si, log2_size)


def _unit_lower_inverses(a_list, ti, si, eye):
    sh = 4
    x = [jnp.where(_same_block(ti, si, sh), a, 0.0) for a in a_list]
    t_inv = [eye + xi for xi in x]
    for _ in range(sh - 1):
        x = [_mm(xi, xi) for xi in x]
        t_inv = [t + _mm(t, xi) for t, xi in zip(t_inv, x)]
    while (1 << sh) < a_list[0].shape[0]:
        off = jnp.logical_and(jnp.logical_not(_same_block(ti, si, sh)), _same_block(ti, si, sh + 1))
        at = [_mm(jnp.where(off, a, 0.0), t) for a, t in zip(a_list, t_inv)]
        t_inv = [t + _mm(t, m) for t, m in zip(t_inv, at)]
        sh += 1
    return t_inv


def _cumsum_rows(tri_bf16, x):
    hi = x.astype(BF16)
    lo = (x - hi.astype(F32)).astype(BF16)
    return (jnp.dot(tri_bf16, hi, preferred_element_type=F32)
            + jnp.dot(tri_bf16, lo, preferred_element_type=F32))


def _rwkv_scan_kernel(*refs, rev, nc, final, pairs):
    if final:
        (r_ref, k_ref, v_ref, a_ref, lw_ref, kk_w_ref, ka_w_ref,
         yf_ref, g_ref, rk_w_ref, lnw_ref, lnb_ref, o_ref, s_ref) = refs
    else:
        r_ref, k_ref, v_ref, a_ref, lw_ref, kk_w_ref, ka_w_ref, o_ref, s_ref = refs
    L = CHUNK

    @pl.when(pl.program_id(2) == 0)
    def _():
        s_ref[...] = jnp.zeros_like(s_ref)

    lane_lo = lax.broadcasted_iota(jnp.int32, (L, LANES), 1) < HEAD_DIM
    ti = lax.broadcasted_iota(jnp.int32, (L, L), 0)
    si = lax.broadcasted_iota(jnp.int32, (L, L), 1)
    tri = (si >= ti) if rev else (si <= ti)
    strict = (si > ti) if rev else (si < ti)
    eye = (si == ti).astype(F32)
    heads = [(q, h) for q in range(pairs) for h in range(2)]
    cols = [slice(q * LANES, (q + 1) * LANES) for q in range(pairs)]
    own = lambda h, x: jnp.where(lane_lo, x, 0.0) if h == 0 else jnp.where(lane_lo, 0.0, x)

    lw_all = lw_ref[0]
    cum_all = _cumsum_rows(jnp.where(tri, 1.0, 0.0).astype(BF16), lw_all)
    r = [r_ref[0, :, c] for c in cols]
    v = [v_ref[0, :, c] for c in cols]
    kmod, a_t, b_t, k_t, r_t, b_h, k_h, tot = [], [], [], [], [], [], [], []
    for q, c in enumerate(cols):
        k, a, cum, lw = k_ref[0, :, c], a_ref[0, :, c], cum_all[:, c], lw_all[:, c]
        kk = k * kk_w_ref[:, c]
        kk = kk / jnp.maximum(jnp.sqrt(_head_sum(kk * kk, lane_lo)), 1e-12)
        km = k * (1.0 + (a - 1.0) * ka_w_ref[:, c])
        ib = kk * a
        tq = cum[0:1, :] if rev else cum[L - 1:L, :]
        p_inv = jnp.exp(-cum)
        p_end = jnp.exp(tq - cum)
        kmod.append(km)
        tot.append(tq)
        a_t.append(-kk * jnp.exp(cum - lw))
        b_t.append(ib * p_inv)
        k_t.append(km * p_inv)
        r_t.append(r[q] * jnp.exp(cum))
        b_h.append(ib * p_end)
        k_h.append(km * p_end)

    a_m = [own(h, a_t[q]) for q, h in heads]
    gram = [_mm_nt(jnp.concatenate([a_m[i], own(h, r_t[q])], axis=0),
                   jnp.concatenate([b_t[q], k_t[q]], axis=0)) for i, (q, h) in enumerate(heads)]
    a_ab = [jnp.where(strict, g[:L, :L], 0.0) for g in gram]
    av = [_mm(jnp.where(strict, g[:L, L:], 0.0), own(h, v[q])) for g, (q, h) in zip(gram, heads)]
    m_rb = [jnp.where(tri, g[L:, :L], 0.0) for g in gram]
    m_rk = [jnp.where(tri, g[L:, L:], 0.0) for g in gram]
    t_inv = _unit_lower_inverses(a_ab, ti, si, eye)

    w12 = [_mm(jnp.concatenate([t_inv[2 * q], t_inv[2 * q + 1]], axis=1),
               jnp.concatenate([jnp.concatenate([a_m[2 * q], av[2 * q]], axis=1),
                                jnp.concatenate([a_m[2 * q + 1], av[2 * q + 1]], axis=1)], axis=0))
           for q in range(pairs)]
    s0 = [s_ref[q] for q in range(pairs)]
    u = [_mm_nt(w12[q][:, :LANES], s0[q]) + w12[q][:, LANES:] for q in range(pairs)]
    y = []
    for q in range(pairs):
        u_lo, v_lo = own(0, u[q]), own(0, v[q])
        uv = jnp.concatenate([u_lo, u[q] - u_lo, v_lo, v[q] - v_lo], axis=0)
        m_cat = jnp.concatenate([m_rb[2 * q], m_rb[2 * q + 1], m_rk[2 * q], m_rk[2 * q + 1]], axis=1)
        y.append(_mm_nt(r_t[q], s0[q]) + _mm(m_cat, uv))
    for q in range(pairs):
        s_new = s0[q] * jnp.exp(tot[q]) + _mm_tn(jnp.concatenate([u[q], v[q]], axis=0),
                                                 jnp.concatenate([b_h[q], k_h[q]], axis=0))
        s_ref[q] = jnp.where(_same_block(ti, si, 6), s_new, 0.0)

    for q, c in enumerate(cols):
        if not final:
            o_ref[0, :, c] = y[q]
        else:
            yq = y[q] + yf_ref[0, :, c]
            inv_n = 1.0 / HEAD_DIM
            yc = yq - _head_sum(yq, lane_lo) * inv_n
            yn = yc * lax.rsqrt(_head_sum(yc * yc, lane_lo) * inv_n + RWKV_LN_EPS)
            bonus = _head_sum(r[q] * kmod[q] * rk_w_ref[:, c], lane_lo)
            o_ref[0, :, c] = (yn * lnw_ref[:, c] + lnb_ref[:, c] + bonus * v[q]) * g_ref[0, :, c]


def _rwkv_scan(rkvs, a, lw, kk_w, ka_w, rev, final_args=None):
    bsz, t, _ = rkvs.shape
    nc = t // CHUNK
    pairs = RWKV_PAIRS_PER_STEP
    w = pairs * LANES
    ngrp = D_MODEL // w
    final = final_args is not None

    def tmap(off):
        if rev:
            return lambda b, p, c: (b, nc - 1 - c, off + p)
        return lambda b, p, c: (b, c, off + p)

    blk = lambda off: pl.BlockSpec((1, CHUNK, w), tmap(off))
    par = pl.BlockSpec((1, w), lambda b, p, c: (0, p))
    in_specs = [blk(0), blk(ngrp), blk(2 * ngrp), blk(0), blk(0), par, par]
    args = [rkvs, rkvs, rkvs, a, lw, kk_w, ka_w]
    if final:
        yf, g, rk_w, ln_w, ln_b = final_args
        in_specs += [blk(0), blk(0), par, par, par]
        args += [yf, g, rk_w, ln_w, ln_b]
    return pl.pallas_call(
        functools.partial(_rwkv_scan_kernel, rev=rev, nc=nc, final=final, pairs=pairs),
        grid=(bsz, ngrp, nc),
        in_specs=in_specs,
        out_specs=blk(0),
        out_shape=jax.ShapeDtypeStruct((bsz, t, D_MODEL), F32),
        scratch_shapes=[pltpu.VMEM((pairs, LANES, LANES), F32)],
        compiler_params=_cparams("parallel", "parallel", "arbitrary"),
        name="rwkv_scan_bwd" if rev else "rwkv_scan_fwd",
    )(*args)


def _ssd_kernel(*refs, rev, nc, final):
    if final:
        (xbc_ref, xp_ref, xn_ref, dt_ref, cw_ref, cb_ref, dtb_ref, alog_ref,
         z_ref, yf_ref, dsk_ref, nw_ref, o_ref, h_ref, pad_ref, y_ref) = refs
    else:
        (xbc_ref, xp_ref, xn_ref, dt_ref, cw_ref, cb_ref, dtb_ref, alog_ref,
         o_ref, h_ref, pad_ref, y_ref) = refs
    L = CHUNK
    c = pl.program_id(1)
    cc = nc - 1 - c if rev else c

    @pl.when(c == 0)
    def _():
        h_ref[...] = jnp.zeros_like(h_ref)

    pad_ref[0:SUBLANES, :] = xp_ref[0] * (cc > 0).astype(F32)
    pad_ref[SUBLANES:SUBLANES + L, :] = xbc_ref[0]
    pad_ref[SUBLANES + L:2 * SUBLANES + L, :] = xn_ref[0] * (cc < nc - 1).astype(F32)
    acc = cb_ref[...] + cw_ref[0:1, :] * pad_ref[SUBLANES - 2:SUBLANES - 2 + L, :]
    for j in range(1, SSD_CONV):
        lo = SUBLANES - 2 + j
        acc = acc + cw_ref[j:j + 1, :] * pad_ref[lo:lo + L, :]
    xa = acc * jax.nn.sigmoid(acc)
    xs = xa[:, :D_MODEL]
    n_bc = SSD_GROUPS * SSD_STATE
    b_all = xa[:, D_MODEL:D_MODEL + n_bc]
    c_all = xa[:, D_MODEL + n_bc:]

    dtv = jax.nn.softplus(dt_ref[0] + dtb_ref[...])
    da = dtv * -jnp.exp(alog_ref[...])
    ti = lax.broadcasted_iota(jnp.int32, (L, L), 0)
    si = lax.broadcasted_iota(jnp.int32, (L, L), 1)
    tri = (si >= ti) if rev else (si <= ti)
    cs = jnp.dot(tri.astype(F32), da, precision=HIGHEST, preferred_element_type=F32)
    cs_t = cs.T
    edge = 0 if rev else L - 1
    hg = N_HEADS // SSD_GROUPS
    for g in range(SSD_GROUPS):
        bm = b_all[:, g * SSD_STATE:(g + 1) * SSD_STATE]
        cm = c_all[:, g * SSD_STATE:(g + 1) * SSD_STATE]
        cb = _mm_nt(cm, bm)
        bm_t = bm.T
        for hh in range(g * hg, (g + 1) * hg):
            col_i = (N_HEADS if rev else 0) + hh
            col = cs[:, col_i:col_i + 1]
            row = cs_t[col_i:col_i + 1, :]
            decay = jnp.exp(jnp.where(tri, col - row, -jnp.inf))
            xdt = xs[:, hh * HEAD_DIM:(hh + 1) * HEAD_DIM] * dtv[:, col_i:col_i + 1]
            h0 = h_ref[hh]
            y = _mm(cb * decay, xdt) + _mm(cm, h0) * jnp.exp(col)
            tot = cs[edge:edge + 1, col_i:col_i + 1]
            h_ref[hh] = h0 * jnp.exp(tot) + _mm(bm_t, xdt * jnp.exp(tot - col))
            y_ref[:, hh * HEAD_DIM:(hh + 1) * HEAD_DIM] = y

    y = y_ref[...]
    if not final:
        o_ref[0] = y
    else:
        z = z_ref[0]
        y = (y + yf_ref[0] + dsk_ref[...] * xs) * (z * jax.nn.sigmoid(z))
        gw = D_MODEL // SSD_GROUPS
        parts = []
        for g in range(SSD_GROUPS):
            yg = y[:, g * gw:(g + 1) * gw]
            parts.append(yg * lax.rsqrt(jnp.mean(yg * yg, axis=-1, keepdims=True) + 1e-6))
        o_ref[0] = jnp.concatenate(parts, axis=1) * nw_ref[...]


def _ssd(u3, conv_w, conv_b, dt_bias, a_log, rev, final_args=None):
    bsz, t, _ = u3.shape
    nc = t // CHUNK
    r8 = CHUNK // SUBLANES
    last8 = t // SUBLANES - 1
    final = final_args is not None
    cmap = (lambda c: nc - 1 - c) if rev else (lambda c: c)

    def const(shape):
        return pl.BlockSpec(shape, lambda b, c: (0,) * len(shape))

    in_specs = [
        pl.BlockSpec((1, CHUNK, EV_XBC_W), lambda b, c: (b, cmap(c), EV_XBC_BLK)),
        pl.BlockSpec((1, SUBLANES, EV_XBC_W), lambda b, c: (b, jnp.maximum(cmap(c) * r8 - 1, 0), EV_XBC_BLK)),
        pl.BlockSpec((1, SUBLANES, EV_XBC_W),
                     lambda b, c: (b, jnp.minimum((cmap(c) + 1) * r8, last8), EV_XBC_BLK)),
        pl.BlockSpec((1, CHUNK, EV_DT_W), lambda b, c: (b, cmap(c), EV_DT_BLK)),
        const((SUBLANES, EV_XBC_W)), const((1, EV_XBC_W)), const((1, LANES)), const((1, LANES)),
    ]
    args = [u3, u3, u3, u3, conv_w, conv_b, dt_bias, a_log]
    if final:
        yf, d_skip, norm_w = final_args
        in_specs += [pl.BlockSpec((1, CHUNK, EV_Z_W), lambda b, c: (b, cmap(c), EV_Z_BLK)),
                     pl.BlockSpec((1, CHUNK, D_MODEL), lambda b, c: (b, cmap(c), 0)),
                     const((1, D_MODEL)), const((1, D_MODEL))]
        args += [u3, yf, d_skip, norm_w]
    return pl.pallas_call(
        functools.partial(_ssd_kernel, rev=rev, nc=nc, final=final),
        grid=(bsz, nc),
        in_specs=in_specs,
        out_specs=pl.BlockSpec((1, CHUNK, D_MODEL), lambda b, c: (b, cmap(c), 0)),
        out_shape=jax.ShapeDtypeStruct((bsz, t, D_MODEL), F32),
        scratch_shapes=[pltpu.VMEM((N_HEADS, SSD_STATE, HEAD_DIM), F32),
                        pltpu.VMEM((CHUNK + 2 * SUBLANES, EV_XBC_W), F32),
                        pltpu.VMEM((CHUNK, D_MODEL), F32)],
        compiler_params=_cparams("parallel", "arbitrary"),
        name="ssd_bwd" if rev else "ssd_fwd",
    )(*args)


def _conv_module_kernel(val_ref, gate_ref, vp_ref, gp_ref, vn_ref, gn_ref,
                        dw_ref, db_ref, lnw_ref, lnb_ref, o_ref, pad_ref, *, nt):
    t = pl.program_id(1)
    tt = val_ref.shape[1]
    h = CONV_HALO
    pad_ref[0:h, :] = vp_ref[0] * jax.nn.sigmoid(gp_ref[0]) * (t > 0).astype(F32)
    pad_ref[h:h + tt, :] = val_ref[0] * jax.nn.sigmoid(gate_ref[0])
    pad_ref[h + tt:2 * h + tt, :] = vn_ref[0] * jax.nn.sigmoid(gn_ref[0]) * (t < nt - 1).astype(F32)
    base = h - CONV_WIDTH // 2
    acc = db_ref[...] + dw_ref[0:1, :] * pad_ref[base:base + tt, :]
    for j in range(1, CONV_WIDTH):
        acc = acc + dw_ref[j:j + 1, :] * pad_ref[base + j:base + j + tt, :]
    xc = acc - jnp.mean(acc, axis=-1, keepdims=True)
    xn = xc * lax.rsqrt(jnp.mean(xc * xc, axis=-1, keepdims=True) + 1e-5)
    yv = xn * lnw_ref[...] + lnb_ref[...]
    o_ref[0] = yv * jax.nn.sigmoid(yv)


def _conv_module(u3, dw_w, dw_b, ln_w, ln_b, tt):
    bsz, t, _ = u3.shape
    nt = t // tt
    rh = tt // CONV_HALO
    lasth = t // CONV_HALO - 1
    d = D_MODEL

    def const(shape):
        return pl.BlockSpec(shape, lambda b, i: (0,) * len(shape))

    main = lambda blk: pl.BlockSpec((1, tt, d), lambda b, i: (b, i, blk))
    prev = lambda blk: pl.BlockSpec((1, CONV_HALO, d), lambda b, i: (b, jnp.maximum(i * rh - 1, 0), blk))
    nxt = lambda blk: pl.BlockSpec((1, CONV_HALO, d), lambda b, i: (b, jnp.minimum((i + 1) * rh, lasth), blk))
    return pl.pallas_call(
        functools.partial(_conv_module_kernel, nt=nt),
        grid=(bsz, nt),
        in_specs=[main(0), main(1), prev(0), prev(1), nxt(0), nxt(1),
                  const((4 * SUBLANES, d)), const((1, d)), const((1, d)), const((1, d))],
        out_specs=pl.BlockSpec((1, tt, d), lambda b, i: (b, i, 0)),
        out_shape=jax.ShapeDtypeStruct((bsz, t, d), F32),
        scratch_shapes=[pltpu.VMEM((tt + 2 * CONV_HALO, d), F32)],
        compiler_params=_cparams("parallel", "parallel"),
        name="conv_module",
    )(u3, u3, u3, u3, u3, u3, dw_w, dw_b, ln_w, ln_b)


def _attention_kernel(q_ref, k0_ref, k1_ref, k2_ref, v0_ref, v1_ref, v2_ref,
                      bias_ref, qw_ref, kw_ref, sink_ref, o_ref, *, nblk):
    i = pl.program_id(1)
    blk = ATT_BLOCK
    col = lax.broadcasted_iota(jnp.int32, (blk, 3 * blk), 1)
    oob = ((i == 0) & (col < blk)) | ((i == nblk - 1) & (col >= 2 * blk))
    scale = HEAD_DIM ** -0.5
    k_all = jnp.concatenate([k0_ref[0], k1_ref[0], k2_ref[0]], axis=0)
    v_all = jnp.concatenate([v0_ref[0], v1_ref[0], v2_ref[0]], axis=0)
    q_all = q_ref[0]
    for kv in range(ATT_KV_HEADS):
        ks = _rms(k_all[:, kv * HEAD_DIM:(kv + 1) * HEAD_DIM], kw_ref[...])
        vs = v_all[:, kv * HEAD_DIM:(kv + 1) * HEAD_DIM]
        for gq in range(ATT_GQ):
            hq = kv * ATT_GQ + gq
            qs = _rms(q_all[:, hq * HEAD_DIM:(hq + 1) * HEAD_DIM], qw_ref[...])
            logits = _mm_nt(qs, ks) * scale + bias_ref[hq]
            logits = jnp.where(oob, -jnp.inf, logits)
            sink = sink_ref[hq]
            m = jnp.maximum(jnp.max(logits, axis=-1, keepdims=True), sink)
            p = jnp.exp(logits - m)
            denom = jnp.sum(p, axis=-1, keepdims=True) + jnp.exp(sink - m)
            o_ref[0, :, hq * HEAD_DIM:(hq + 1) * HEAD_DIM] = _mm(p, vs) / denom


def _attention(u3, bias, q_norm_w, k_norm_w, sink):
    bsz, t, _ = u3.shape
    nblk = t // ATT_BLOCK
    d = D_MODEL
    kvw = ATT_KV_HEADS * HEAD_DIM
    kblk = (2 * D_MODEL + d) // kvw
    vblk = kblk + 1

    def kv_spec(blk, off):
        return pl.BlockSpec((1, ATT_BLOCK, kvw), lambda b, i: (b, jnp.clip(i + off, 0, nblk - 1), blk))

    def const(shape):
        return pl.BlockSpec(shape, lambda b, i: (0,) * len(shape))

    return pl.pallas_call(
        functools.partial(_attention_kernel, nblk=nblk),
        grid=(bsz, nblk),
        in_specs=[pl.BlockSpec((1, ATT_BLOCK, d), lambda b, i: (b, i, 2)),
                  kv_spec(kblk, -1), kv_spec(kblk, 0), kv_spec(kblk, 1),
                  kv_spec(vblk, -1), kv_spec(vblk, 0), kv_spec(vblk, 1),
                  const((N_HEADS, ATT_BLOCK, 3 * ATT_BLOCK)), const((1, HEAD_DIM)), const((1, HEAD_DIM)),
                  pl.BlockSpec(memory_space=pltpu.SMEM)],
        out_specs=pl.BlockSpec((1, ATT_BLOCK, d), lambda b, i: (b, i, 0)),
        out_shape=jax.ShapeDtypeStruct((bsz, t, d), F32),
        compiler_params=_cparams("parallel", "parallel"),
        name="attention",
    )(u3, u3, u3, u3, u3, u3, u3, bias, q_norm_w, k_norm_w, sink)


def _t5_bucket(rel):
    nb = REL_BUCKETS // 2
    max_exact = nb // 2
    n = jnp.abs(rel)
    nf = jnp.maximum(n, 1).astype(jnp.float32)
    large = max_exact + (jnp.log(nf / max_exact) / math.log(REL_MAX_DIST / max_exact)
                         * (nb - max_exact)).astype(jnp.int32)
    large = jnp.minimum(large, nb - 1)
    return (rel > 0).astype(jnp.int32) * nb + jnp.where(n < max_exact, n, large)


def _attention_bias(rel_bias):
    rel = jnp.arange(3 * ATT_BLOCK)[None, :] - ATT_BLOCK - jnp.arange(ATT_BLOCK)[:, None]
    bias = rel_bias.astype(F32)[_t5_bucket(rel)].transpose(2, 0, 1)
    return jnp.where((jnp.abs(rel) <= ATT_BLOCK)[None], bias, -jnp.inf)


def _pad_cols(w, width):
    return jnp.pad(w, ((0, 0), (0, width - w.shape[1])))


def _even_in_weight(w_in):
    z = w_in[:, :D_MODEL]
    xbc = w_in[:, D_MODEL:D_MODEL + SSD_XBC]
    dt = w_in[:, D_MODEL + SSD_XBC:D_MODEL + SSD_XBC + 2 * N_HEADS]
    p = w_in[:, D_MODEL + SSD_XBC + 2 * N_HEADS:]
    w = jnp.concatenate([p[:, :EV_RKV_W], xbc, p[:, EV_RKV_W:], _pad_cols(dt, 2 * LANES), z], axis=1)
    assert w.shape[1] == EV_W
    return w.astype(BF16)


def _row(v):
    return v.reshape(1, -1).astype(F32)


def _even_params(e, ev_w_in, ev_w_out, ssd_conv_w, ssd_conv_b, ssd_dt_bias, ssd_a_log, ssd_d, ssd_norm_w,
                 rwkv_mu, rwkv_w0, rwkv_w2, rwkv_a0, rwkv_a2, rwkv_g2, rwkv_k_k, rwkv_k_a, rwkv_r_k,
                 rwkv_ln_w, rwkv_ln_b):
    zeros = lambda n: jnp.zeros((n, D_MODEL), F32)
    lr_pad = lambda w, lo: jnp.concatenate([zeros(lo), w.astype(F32), zeros(EV_LR_W - lo - w.shape[0])], 0).astype(BF16)
    return dict(
        w_in=_even_in_weight(ev_w_in[e]),
        w_out=ev_w_out[e].astype(BF16),
        conv_w=jnp.pad(ssd_conv_w[e].astype(F32), ((0, SUBLANES - SSD_CONV), (0, 0))),
        conv_b=_row(ssd_conv_b[e]),
        dt_bias=_pad_cols(_row(ssd_dt_bias[e]), LANES),
        a_log=_pad_cols(_row(ssd_a_log[e]), LANES),
        d_skip=_row(jnp.repeat(ssd_d[e], HEAD_DIM)),
        ssd_norm_w=_row(ssd_norm_w[e]),
        mu_rkv=_row(rwkv_mu[e, :EV_RKV_W]),
        mu_lr=_row(rwkv_mu[e, EV_RKV_W:]),
        a0=_row(rwkv_a0[e]),
        w0=rwkv_w0[e].reshape(2, 1, D_MODEL).astype(F32),
        w2=jnp.stack([lr_pad(rwkv_w2[e, d], 0) for d in range(2)]),
        a2=lr_pad(rwkv_a2[e], 64),
        g2=lr_pad(rwkv_g2[e], 128),
        k_k=_row(rwkv_k_k[e]), k_a=_row(rwkv_k_a[e]), r_k=_row(rwkv_r_k[e]),
        ln_w=_row(rwkv_ln_w[e]), ln_b=_row(rwkv_ln_b[e]),
    )


def _even_layer(x3, norm_w, p, tm):
    bsz, t, d = x3.shape
    x2 = x3.reshape(bsz * t, d)
    u3 = _norm_matmul(x2, norm_w, p["w_in"], tm, 1024).reshape(bsz, t, EV_W)
    ssd_args = (u3, p["conv_w"], p["conv_b"], p["dt_bias"], p["a_log"])
    ya_f = _ssd(*ssd_args, rev=False)
    ya = _ssd(*ssd_args, rev=True, final_args=(ya_f, p["d_skip"], p["ssd_norm_w"]))
    rkvs, a, lw0, lw1, g = _rwkv_prep(u3, p["mu_rkv"], p["mu_lr"], p["a0"], p["a2"], p["w0"], p["w2"], p["g2"],
                                      min(t, 256))
    yb_f = _rwkv_scan(rkvs, a, lw0, p["k_k"], p["k_a"], rev=False)
    yb = _rwkv_scan(rkvs, a, lw1, p["k_k"], p["k_a"], rev=True,
                    final_args=(yb_f, g, p["r_k"], p["ln_w"], p["ln_b"]))
    out = _proj_res(x2, ya.reshape(bsz * t, d), yb.reshape(bsz * t, d), p["w_out"], tm)
    return out.reshape(bsz, t, d)


def _odd_layer(x3, norm_w, p, bias, tm):
    bsz, t, d = x3.shape
    x2 = x3.reshape(bsz * t, d)
    u3 = _norm_matmul(x2, norm_w, p["w_in"], tm, 512).reshape(bsz, t, -1)
    yc = _conv_module(u3, p["dw_w"], p["dw_b"], p["ln_w"], p["ln_b"], min(t, 256))
    yd = _attention(u3, bias, p["q_w"], p["k_w"], p["sink"])
    out = _proj_res(x2, yc.reshape(bsz * t, d), yd.reshape(bsz * t, d), p["w_out"], tm)
    return out.reshape(bsz, t, d)


def _trunk(x3, layers, bias):
    bsz, t, d = x3.shape
    tm = min(bsz * t, 512)
    for kind, norm_mix, p, norm_ffn, ffn_wi, ffn_wo in layers:
        if kind == "even":
            x3 = _even_layer(x3, norm_mix, p, tm)
        else:
            x3 = _odd_layer(x3, norm_mix, p, bias, tm)
        x3 = _ffn(x3.reshape(bsz * t, d), norm_ffn, ffn_wi, ffn_wo, tm).reshape(bsz, t, d)
    return x3


def kernel(x_prompt, x_sample, rel_bias, norm_mix_w, norm_ffn_w, ffn_w_in, ffn_w_out, ev_w_in, ev_w_out, ssd_conv_w, ssd_conv_b, ssd_dt_bias, ssd_a_log, ssd_d, ssd_norm_w, rwkv_mu, rwkv_w0, rwkv_w2, rwkv_a0, rwkv_a2, rwkv_g2, rwkv_k_k, rwkv_k_a, rwkv_r_k, rwkv_ln_w, rwkv_ln_b, od_w_in, od_w_out, conv_dw_w, conv_dw_b, conv_ln_w, conv_ln_b, att_q_norm_w, att_k_norm_w, att_sink):
    depth = norm_mix_w.shape[0]
    bias = _attention_bias(rel_bias)
    layers = []
    for layer in range(depth):
        i = layer // 2
        if layer % 2 == 0:
            kind = "even"
            p = _even_params(i, ev_w_in, ev_w_out, ssd_conv_w, ssd_conv_b, ssd_dt_bias, ssd_a_log, ssd_d,
                             ssd_norm_w, rwkv_mu, rwkv_w0, rwkv_w2, rwkv_a0, rwkv_a2, rwkv_g2, rwkv_k_k,
                             rwkv_k_a, rwkv_r_k, rwkv_ln_w, rwkv_ln_b)
        else:
            kind = "odd"
            p = dict(
                w_in=od_w_in[i].astype(BF16), w_out=od_w_out[i].astype(BF16),
                dw_w=jnp.pad(conv_dw_w[i].astype(F32), ((0, 4 * SUBLANES - CONV_WIDTH), (0, 0))),
                dw_b=_row(conv_dw_b[i]), ln_w=_row(conv_ln_w[i]), ln_b=_row(conv_ln_b[i]),
                q_w=_row(att_q_norm_w[i]), k_w=_row(att_k_norm_w[i]), sink=att_sink[i].astype(F32),
            )
        layers.append((kind, _row(norm_mix_w[layer]), p, _row(norm_ffn_w[layer]),
                       ffn_w_in[layer].astype(BF16), ffn_w_out[layer].astype(BF16)))
    return (_trunk(x_prompt, layers, bias), _trunk(x_sample, layers, bias))
```

```python
import functools
import math

import jax
import jax.numpy as jnp
import numpy as np
from jax import lax
from jax.experimental import pallas as pl
from jax.experimental.pallas import tpu as pltpu

F32 = jnp.float32
BF16 = jnp.bfloat16
HIGHEST = lax.Precision.HIGHEST

LANES = 128
SUBLANES = 8
VMEM_LIMIT_BYTES = 56 * 1024 * 1024

D_MODEL = 1024
HEAD_DIM = 64
D_FF = 2816
FF_CHUNK = 256
NORM_MM_ROWS = 2048
N_HEADS = 16
SSD_GROUPS = 2
SSD_STATE = 128
SSD_CONV = 5
SSD_XBC = D_MODEL + 2 * SSD_GROUPS * SSD_STATE
CHUNK = 128
RWKV_LN_EPS = 64e-5
RWKV_PAIRS_PER_STEP = 8
CONV_WIDTH = 31
CONV_HALO = 16
ATT_BLOCK = 128
ATT_KV_HEADS = 4
ATT_GQ = 4
REL_BUCKETS = 32
REL_MAX_DIST = 128

EV_W = 6144
EV_RKV_W, EV_XBC_W, EV_LR_W, EV_DT_W, EV_Z_W = 3072, SSD_XBC, 256, LANES, D_MODEL
EV_XBC_BLK = 3072 // EV_XBC_W
EV_LR_BLK = 4608 // EV_LR_W
EV_DT_BLK = 4864 // EV_DT_W
EV_Z_BLK = 5120 // EV_Z_W


def _cparams(*sem):
    return pltpu.CompilerParams(dimension_semantics=sem, vmem_limit_bytes=VMEM_LIMIT_BYTES)


def _mm(a, b):
    return jnp.dot(a.astype(BF16), b.astype(BF16), preferred_element_type=F32)


def _mm_nt(a, b):
    return lax.dot_general(a.astype(BF16), b.astype(BF16), (((1,), (1,)), ((), ())),
                           preferred_element_type=F32)


def _mm_tn(a, b):
    return lax.dot_general(a.astype(BF16), b.astype(BF16), (((0,), (0,)), ((), ())),
                           preferred_element_type=F32)


def _rms(x, w, eps=1e-6):
    return x * lax.rsqrt(jnp.mean(x * x, axis=-1, keepdims=True) + eps) * w


def _norm_matmul_kernel(x_ref, nw_ref, w_ref, o_ref, h_ref):
    @pl.when(pl.program_id(1) == 0)
    def _():
        h_ref[...] = _rms(x_ref[...], nw_ref[...]).astype(BF16)

    o_ref[...] = jnp.dot(h_ref[...], w_ref[...], preferred_element_type=F32)


def _norm_matmul(x2, nw, w, tm, tn):
    n, d = x2.shape
    nout = w.shape[1]
    return pl.pallas_call(
        _norm_matmul_kernel,
        grid=(n // tm, nout // tn),
        in_specs=[pl.BlockSpec((tm, d), lambda i, j: (i, 0)),
                  pl.BlockSpec((1, d), lambda i, j: (0, 0)),
                  pl.BlockSpec((d, tn), lambda i, j: (0, j))],
        out_specs=pl.BlockSpec((tm, tn), lambda i, j: (i, j)),
        out_shape=jax.ShapeDtypeStruct((n, nout), F32),
        scratch_shapes=[pltpu.VMEM((tm, d), BF16)],
        compiler_params=_cparams("parallel", "arbitrary"),
        name="norm_matmul",
    )(x2, nw, w)


def _proj_res_kernel(x_ref, a_ref, b_ref, wa_ref, wb_ref, o_ref):
    o_ref[...] = x_ref[...] + _mm(a_ref[...], wa_ref[...]) + _mm(b_ref[...], wb_ref[...])


def _proj_res(x2, a2, b2, w, tm):
    n, d = x2.shape
    row = pl.BlockSpec((tm, d), lambda i: (i, 0))
    return pl.pallas_call(
        _proj_res_kernel,
        grid=(n // tm,),
        in_specs=[row, row, row,
                  pl.BlockSpec((d, d), lambda i: (0, 0)),
                  pl.BlockSpec((d, d), lambda i: (1, 0))],
        out_specs=row,
        out_shape=jax.ShapeDtypeStruct((n, d), F32),
        compiler_params=_cparams("parallel"),
        name="proj_res",
    )(x2, a2, b2, w, w)


def _ffn_kernel(x_ref, nw_ref, wi_ref, wo_ref, o_ref):
    x = x_ref[...]
    h = _rms(x, nw_ref[...]).astype(BF16)
    o_ref[...] = x
    for c in range(D_FF // FF_CHUNK):
        lo = c * FF_CHUNK
        g = jnp.dot(h, wi_ref[:, lo:lo + FF_CHUNK], preferred_element_type=F32)
        u = jnp.dot(h, wi_ref[:, D_FF + lo:D_FF + lo + FF_CHUNK], preferred_element_type=F32)
        act = (g * jax.nn.sigmoid(g) * u).astype(BF16)
        o_ref[...] += jnp.dot(act, wo_ref[lo:lo + FF_CHUNK, :], preferred_element_type=F32)


def _ffn(x2, nw, wi, wo, tm):
    n, d = x2.shape
    row = pl.BlockSpec((tm, d), lambda i: (i, 0))
    return pl.pallas_call(
        _ffn_kernel,
        grid=(n // tm,),
        in_specs=[row,
                  pl.BlockSpec((1, d), lambda i: (0, 0)),
                  pl.BlockSpec((d, 2 * D_FF), lambda i: (0, 0), pipeline_mode=pl.Buffered(1)),
                  pl.BlockSpec((D_FF, d), lambda i: (0, 0), pipeline_mode=pl.Buffered(1))],
        out_specs=row,
        out_shape=jax.ShapeDtypeStruct((n, d), F32),
        compiler_params=_cparams("parallel"),
        name="ffn",
    )(x2, nw, wi, wo)


def _shift_mix(p, prev_row, next_row, mu):
    tt = p.shape[0]
    rows = lax.broadcasted_iota(jnp.int32, p.shape, 0)
    prev = jnp.where(rows == 0, prev_row, pltpu.roll(p, 1, 0))
    nxt = jnp.where(rows == tt - 1, next_row, pltpu.roll(p, tt - 1, 0))
    return p + mu * (0.5 * (prev + nxt) - p)


def _rwkv_prep_kernel(rkv_ref, rkv_p_ref, rkv_n_ref, lr_ref, lr_p_ref, lr_n_ref,
                      mu_rkv_ref, mu_lr_ref, a0_ref, a2_ref, w0_ref, w2_ref, g2_ref,
                      rkvs_ref, a_ref, lw0_ref, lw1_ref, g_ref, *, nt):
    t = pl.program_id(1)
    keep_p = (t > 0).astype(F32)
    keep_n = (t < nt - 1).astype(F32)
    rkvs_ref[0] = _shift_mix(rkv_ref[0], rkv_p_ref[0, 7:8, :] * keep_p, rkv_n_ref[0, 0:1, :] * keep_n,
                             mu_rkv_ref[...])
    lr = _shift_mix(lr_ref[0], lr_p_ref[0, 7:8, :] * keep_p, lr_n_ref[0, 0:1, :] * keep_n, mu_lr_ref[...])
    a_ref[0] = jax.nn.sigmoid(a0_ref[...] + _mm(lr, a2_ref[...]))
    g_ref[0] = _mm(jax.nn.sigmoid(lr), g2_ref[...])
    th = jnp.tanh(lr)
    for d, out in enumerate((lw0_ref, lw1_ref)):
        wlog = -jax.nn.softplus(-(w0_ref[d] + _mm(th, w2_ref[d]))) - 0.5
        out[0] = -jnp.exp(wlog)


def _rwkv_prep(u3, mu_rkv, mu_lr, a0, a2p, w0, w2p, g2p, tt):
    bsz, t, _ = u3.shape
    nt = t // tt
    r8 = tt // SUBLANES
    last8 = t // SUBLANES - 1
    d = D_MODEL

    def main(w, blk):
        return pl.BlockSpec((1, tt, w), lambda b, i: (b, i, blk))

    def prev(w, blk):
        return pl.BlockSpec((1, SUBLANES, w), lambda b, i: (b, jnp.maximum(i * r8 - 1, 0), blk))

    def nxt(w, blk):
        return pl.BlockSpec((1, SUBLANES, w), lambda b, i: (b, jnp.minimum((i + 1) * r8, last8), blk))

    def const(shape):
        return pl.BlockSpec(shape, lambda b, i: (0,) * len(shape))

    out = pl.BlockSpec((1, tt, d), lambda b, i: (b, i, 0))
    sds = jax.ShapeDtypeStruct((bsz, t, d), F32)
    return pl.pallas_call(
        functools.partial(_rwkv_prep_kernel, nt=nt),
        grid=(bsz, nt),
        in_specs=[main(EV_RKV_W, 0), prev(EV_RKV_W, 0), nxt(EV_RKV_W, 0),
                  main(EV_LR_W, EV_LR_BLK), prev(EV_LR_W, EV_LR_BLK), nxt(EV_LR_W, EV_LR_BLK),
                  const((1, EV_RKV_W)), const((1, EV_LR_W)), const((1, d)), const((EV_LR_W, d)),
                  const((2, 1, d)), const((2, EV_LR_W, d)), const((EV_LR_W, d))],
        out_specs=[pl.BlockSpec((1, tt, EV_RKV_W), lambda b, i: (b, i, 0)), out, out, out, out],
        out_shape=[jax.ShapeDtypeStruct((bsz, t, EV_RKV_W), F32), sds, sds, sds, sds],
        compiler_params=_cparams("parallel", "parallel"),
        name="rwkv_prep",
    )(u3, u3, u3, u3, u3, u3, mu_rkv, mu_lr, a0, a2p, w0, w2p, g2p)


def _head_sum(x, lane_lo):
    s0 = jnp.sum(jnp.where(lane_lo, x, 0.0), axis=-1, keepdims=True)
    s1 = jnp.sum(jnp.where(lane_lo, 0.0, x), axis=-1, keepdims=True)
    return jnp.where(lane_lo, s0, s1)


def _same_block(ti, si, log2_size):
    return jnp.right_shift(ti, log2_size) == jnp.right_shift(si, log2_size)


def _unit_lower_inverses(a_list, ti, si, eye):
    sh = 4
    x = [jnp.where(_same_block(ti, si, sh), a, 0.0) for a in a_list]
    t_inv = [eye + xi for xi in x]
    for _ in range(sh - 1):
        x = [_mm(xi, xi) for xi in x]
        t_inv = [t + _mm(t, xi) for t, xi in zip(t_inv, x)]
    while (1 << sh) < a_list[0].shape[0]:
        off = jnp.logical_and(jnp.logical_not(_same_block(ti, si, sh)), _same_block(ti, si, sh + 1))
        at = [_mm(jnp.where(off, a, 0.0), t) for a, t in zip(a_list, t_inv)]
        t_inv = [t + _mm(t, m) for t, m in zip(t_inv, at)]
        sh += 1
    return t_inv


def _cumsum_rows(tri_bf16, x):
    hi = x.astype(BF16)
    lo = (x - hi.astype(F32)).astype(BF16)
    return (jnp.dot(tri_bf16, hi, preferred_element_type=F32)
            + jnp.dot(tri_bf16, lo, preferred_element_type=F32))


def _rwkv_scan_kernel(*refs, rev, nc, final, pairs):
    if final:
        (r_ref, k_ref, v_ref, a_ref, lw_ref, kk_w_ref, ka_w_ref,
         yf_ref, g_ref, rk_w_ref, lnw_ref, lnb_ref, o_ref, s_ref) = refs
    else:
        r_ref, k_ref, v_ref, a_ref, lw_ref, kk_w_ref, ka_w_ref, o_ref, s_ref = refs
    L = CHUNK

    @pl.when(pl.program_id(2) == 0)
    def _():
        s_ref[...] = jnp.zeros_like(s_ref)

    lane_lo = lax.broadcasted_iota(jnp.int32, (L, LANES), 1) < HEAD_DIM
    ti = lax.broadcasted_iota(jnp.int32, (L, L), 0)
    si = lax.broadcasted_iota(jnp.int32, (L, L), 1)
    tri = (si >= ti) if rev else (si <= ti)
    strict = (si > ti) if rev else (si < ti)
    eye = (si == ti).astype(F32)
    heads = [(q, h) for q in range(pairs) for h in range(2)]
    cols = [slice(q * LANES, (q + 1) * LANES) for q in range(pairs)]
    own = lambda h, x: jnp.where(lane_lo, x, 0.0) if h == 0 else jnp.where(lane_lo, 0.0, x)

    lw_all = lw_ref[0]
    cum_all = _cumsum_rows(jnp.where(tri, 1.0, 0.0).astype(BF16), lw_all)
    r = [r_ref[0, :, c] for c in cols]
    v = [v_ref[0, :, c] for c in cols]
    kmod, a_t, b_t, k_t, r_t, b_h, k_h, tot = [], [], [], [], [], [], [], []
    for q, c in enumerate(cols):
        k, a, cum, lw = k_ref[0, :, c], a_ref[0, :, c], cum_all[:, c], lw_all[:, c]
        kk = k * kk_w_ref[:, c]
        kk = kk / jnp.maximum(jnp.sqrt(_head_sum(kk * kk, lane_lo)), 1e-12)
        km = k * (1.0 + (a - 1.0) * ka_w_ref[:, c])
        ib = kk * a
        tq = cum[0:1, :] if rev else cum[L - 1:L, :]
        p_inv = jnp.exp(-cum)
        p_end = jnp.exp(tq - cum)
        kmod.append(km)
        tot.append(tq)
        a_t.append(-kk * jnp.exp(cum - lw))
        b_t.append(ib * p_inv)
        k_t.append(km * p_inv)
        r_t.append(r[q] * jnp.exp(cum))
        b_h.append(ib * p_end)
        k_h.append(km * p_end)

    a_m = [own(h, a_t[q]) for q, h in heads]
    gram = [_mm_nt(jnp.concatenate([a_m[i], own(h, r_t[q])], axis=0),
                   jnp.concatenate([b_t[q], k_t[q]], axis=0)) for i, (q, h) in enumerate(heads)]
    a_ab = [jnp.where(strict, g[:L, :L], 0.0) for g in gram]
    av = [_mm(jnp.where(strict, g[:L, L:], 0.0), own(h, v[q])) for g, (q, h) in zip(gram, heads)]
    m_rb = [jnp.where(tri, g[L:, :L], 0.0) for g in gram]
    m_rk = [jnp.where(tri, g[L:, L:], 0.0) for g in gram]
    t_inv = _unit_lower_inverses(a_ab, ti, si, eye)

    w12 = [_mm(jnp.concatenate([t_inv[2 * q], t_inv[2 * q + 1]], axis=1),
               jnp.concatenate([jnp.concatenate([a_m[2 * q], av[2 * q]], axis=1),
                                jnp.concatenate([a_m[2 * q + 1], av[2 * q + 1]], axis=1)], axis=0))
           for q in range(pairs)]
    s0 = [s_ref[q] for q in range(pairs)]
    u = [_mm_nt(w12[q][:, :LANES], s0[q]) + w12[q][:, LANES:] for q in range(pairs)]
    y = []
    for q in range(pairs):
        u_lo, v_lo = own(0, u[q]), own(0, v[q])
        uv = jnp.concatenate([u_lo, u[q] - u_lo, v_lo, v[q] - v_lo], axis=0)
        m_cat = jnp.concatenate([m_rb[2 * q], m_rb[2 * q + 1], m_rk[2 * q], m_rk[2 * q + 1]], axis=1)
        y.append(_mm_nt(r_t[q], s0[q]) + _mm(m_cat, uv))
    for q in range(pairs):
        s_new = s0[q] * jnp.exp(tot[q]) + _mm_tn(jnp.concatenate([u[q], v[q]], axis=0),
                                                 jnp.concatenate([b_h[q], k_h[q]], axis=0))
        s_ref[q] = jnp.where(_same_block(ti, si, 6), s_new, 0.0)

    for q, c in enumerate(cols):
        if not final:
            o_ref[0, :, c] = y[q]
        else:
            yq = y[q] + yf_ref[0, :, c]
            inv_n = 1.0 / HEAD_DIM
            yc = yq - _head_sum(yq, lane_lo) * inv_n
            yn = yc * lax.rsqrt(_head_sum(yc * yc, lane_lo) * inv_n + RWKV_LN_EPS)
            bonus = _head_sum(r[q] * kmod[q] * rk_w_ref[:, c], lane_lo)
            o_ref[0, :, c] = (yn * lnw_ref[:, c] + lnb_ref[:, c] + bonus * v[q]) * g_ref[0, :, c]


def _rwkv_scan(rkvs, a, lw, kk_w, ka_w, rev, final_args=None):
    bsz, t, _ = rkvs.shape
    nc = t // CHUNK
    pairs = RWKV_PAIRS_PER_STEP
    w = pairs * LANES
    ngrp = D_MODEL // w
    final = final_args is not None

    def tmap(off):
        if rev:
            return lambda b, p, c: (b, nc - 1 - c, off + p)
        return lambda b, p, c: (b, c, off + p)

    blk = lambda off: pl.BlockSpec((1, CHUNK, w), tmap(off))
    par = pl.BlockSpec((1, w), lambda b, p, c: (0, p))
    in_specs = [blk(0), blk(ngrp), blk(2 * ngrp), blk(0), blk(0), par, par]
    args = [rkvs, rkvs, rkvs, a, lw, kk_w, ka_w]
    if final:
        yf, g, rk_w, ln_w, ln_b = final_args
        in_specs += [blk(0), blk(0), par, par, par]
        args += [yf, g, rk_w, ln_w, ln_b]
    return pl.pallas_call(
        functools.partial(_rwkv_scan_kernel, rev=rev, nc=nc, final=final, pairs=pairs),
        grid=(bsz, ngrp, nc),
        in_specs=in_specs,
        out_specs=blk(0),
        out_shape=jax.ShapeDtypeStruct((bsz, t, D_MODEL), F32),
        scratch_shapes=[pltpu.VMEM((pairs, LANES, LANES), F32)],
        compiler_params=_cparams("parallel", "parallel", "arbitrary"),
        name="rwkv_scan_bwd" if rev else "rwkv_scan_fwd",
    )(*args)


def _split_dot(x, w_bf16, terms, lhs=False):
    out, rem = None, x
    for _ in range(terms):
        piece = rem.astype(BF16)
        rem = rem - piece.astype(F32)
        part = (jnp.dot(w_bf16, piece, preferred_element_type=F32) if lhs
                else jnp.dot(piece, w_bf16, preferred_element_type=F32))
        out = part if out is None else out + part
    return out


def _ssd_kernel(*refs, rev, nc):
    if rev:
        (xa_ref, dt_ref, dtb_ref, alog_ref, ex_ref, z_ref, yf_ref, dsk_ref, nw_ref, o_ref, h_ref) = refs
    else:
        (xbc_ref, xp_ref, xn_ref, dt_ref, cw_ref, cb_ref, dtb_ref, alog_ref, ex_ref,
         o_ref, xa_out_ref, h_ref, pad_ref) = refs
    L = CHUNK
    c = pl.program_id(1)
    cc = nc - 1 - c if rev else c

    @pl.when(c == 0)
    def _():
        h_ref[...] = jnp.zeros_like(h_ref)

    if rev:
        xa = xa_ref[0]
    else:
        pad_ref[0:SUBLANES, :] = xp_ref[0] * (cc > 0).astype(F32)
        pad_ref[SUBLANES:SUBLANES + L, :] = xbc_ref[0]
        pad_ref[SUBLANES + L:2 * SUBLANES + L, :] = xn_ref[0] * (cc < nc - 1).astype(F32)
        acc = cb_ref[...] + cw_ref[0:1, :] * pad_ref[SUBLANES - 2:SUBLANES - 2 + L, :]
        for j in range(1, SSD_CONV):
            lo = SUBLANES - 2 + j
            acc = acc + cw_ref[j:j + 1, :] * pad_ref[lo:lo + L, :]
        xa = acc * jax.nn.sigmoid(acc)
        xa_out_ref[0] = xa
    xs = xa[:, :D_MODEL]
    n_bc = SSD_GROUPS * SSD_STATE
    gw = D_MODEL // SSD_GROUPS

    dtv = jax.nn.softplus(dt_ref[0] + dtb_ref[...])
    da = dtv * -jnp.exp(alog_ref[...])
    ti = lax.broadcasted_iota(jnp.int32, (L, L), 0)
    si = lax.broadcasted_iota(jnp.int32, (L, L), 1)
    tri = (si >= ti) if rev else (si <= ti)
    cs = _split_dot(da, jnp.where(tri, 1.0, 0.0).astype(BF16), 3, lhs=True)
    cs_t = cs.T
    edge = 0 if rev else L - 1
    ex = ex_ref[...]
    dt_e = _split_dot(dtv, ex, 2)
    cs_e = _split_dot(cs, ex, 3)
    tot_e = cs_e[edge:edge + 1, :]
    xd = xs * dt_e
    xst = xd * jnp.exp(tot_e - cs_e)
    ecs = jnp.exp(cs_e)

    lane_lo = lax.broadcasted_iota(jnp.int32, (L, LANES), 1) < HEAD_DIM
    bm = [xa[:, D_MODEL + g * SSD_STATE:D_MODEL + (g + 1) * SSD_STATE] for g in range(SSD_GROUPS)]
    cm = [xa[:, D_MODEL + n_bc + g * SSD_STATE:D_MODEL + n_bc + (g + 1) * SSD_STATE] for g in range(SSD_GROUPS)]
    cb = [_mm_nt(cm[g], bm[g]) for g in range(SSD_GROUPS)]
    h0 = [h_ref[g] for g in range(SSD_GROUPS)]
    y_off = [_mm(cm[g], h0[g]) for g in range(SSD_GROUPS)]
    st = [_mm(bm[g].T, xst[:, g * gw:(g + 1) * gw]) for g in range(SSD_GROUPS)]
    for g in range(SSD_GROUPS):
        h_ref[g] = h0[g] * jnp.exp(tot_e[:, g * gw:(g + 1) * gw]) + st[g]

    off = N_HEADS if rev else 0
    gmat = []
    for hh in range(N_HEADS):
        col = cs[:, off + hh:off + hh + 1]
        row = cs_t[off + hh:off + hh + 1, :]
        decay = jnp.exp(jnp.where(tri, col - row, -jnp.inf))
        gmat.append((cb[hh // (N_HEADS // SSD_GROUPS)] * decay).astype(BF16))
    y_diag = []
    for p in range(N_HEADS // 2):
        xp = xd[:, p * LANES:(p + 1) * LANES]
        x_lo = jnp.where(lane_lo, xp, 0.0)
        rhs = jnp.concatenate([x_lo, xp - x_lo], axis=0)
        y_diag.append(_mm(jnp.concatenate([gmat[2 * p], gmat[2 * p + 1]], axis=1), rhs))
    y = jnp.concatenate(y_diag, axis=1) + jnp.concatenate(y_off, axis=1) * ecs

    if not rev:
        o_ref[0] = y
    else:
        z = z_ref[0]
        y = (y + yf_ref[0] + dsk_ref[...] * xs) * (z * jax.nn.sigmoid(z))
        parts = []
        for g in range(SSD_GROUPS):
            yg = y[:, g * gw:(g + 1) * gw]
            parts.append(yg * lax.rsqrt(jnp.mean(yg * yg, axis=-1, keepdims=True) + 1e-6))
        o_ref[0] = jnp.concatenate(parts, axis=1) * nw_ref[...]


def _head_expand_matrix(rev):
    e = np.zeros((LANES, D_MODEL), np.float32)
    for hh in range(N_HEADS):
        e[(N_HEADS if rev else 0) + hh, hh * HEAD_DIM:(hh + 1) * HEAD_DIM] = 1.0
    return jnp.asarray(e, dtype=BF16)


def _ssd_fwd(u3, conv_w, conv_b, dt_bias, a_log):
    bsz, t, _ = u3.shape
    nc = t // CHUNK
    r8 = CHUNK // SUBLANES
    last8 = t // SUBLANES - 1

    def const(shape):
        return pl.BlockSpec(shape, lambda b, c: (0,) * len(shape))

    return pl.pallas_call(
        functools.partial(_ssd_kernel, rev=False, nc=nc),
        grid=(bsz, nc),
        in_specs=[
            pl.BlockSpec((1, CHUNK, EV_XBC_W), lambda b, c: (b, c, EV_XBC_BLK)),
            pl.BlockSpec((1, SUBLANES, EV_XBC_W), lambda b, c: (b, jnp.maximum(c * r8 - 1, 0), EV_XBC_BLK)),
            pl.BlockSpec((1, SUBLANES, EV_XBC_W), lambda b, c: (b, jnp.minimum((c + 1) * r8, last8), EV_XBC_BLK)),
            pl.BlockSpec((1, CHUNK, EV_DT_W), lambda b, c: (b, c, EV_DT_BLK)),
            const((SUBLANES, EV_XBC_W)), const((1, EV_XBC_W)), const((1, LANES)), const((1, LANES)),
            const((LANES, D_MODEL)),
        ],
        out_specs=[pl.BlockSpec((1, CHUNK, D_MODEL), lambda b, c: (b, c, 0)),
                   pl.BlockSpec((1, CHUNK, EV_XBC_W), lambda b, c: (b, c, 0))],
        out_shape=[jax.ShapeDtypeStruct((bsz, t, D_MODEL), F32), jax.ShapeDtypeStruct((bsz, t, EV_XBC_W), F32)],
        scratch_shapes=[pltpu.VMEM((SSD_GROUPS, SSD_STATE, D_MODEL // SSD_GROUPS), F32),
                        pltpu.VMEM((CHUNK + 2 * SUBLANES, EV_XBC_W), F32)],
        compiler_params=_cparams("parallel", "arbitrary"),
        name="ssd_fwd",
    )(u3, u3, u3, u3, conv_w, conv_b, dt_bias, a_log, _head_expand_matrix(False))


def _ssd_bwd(u3, xa, yf, dt_bias, a_log, d_skip, norm_w):
    bsz, t, _ = u3.shape
    nc = t // CHUNK

    def const(shape):
        return pl.BlockSpec(shape, lambda b, c: (0,) * len(shape))

    def chunk(w, blk):
        return pl.BlockSpec((1, CHUNK, w), lambda b, c: (b, nc - 1 - c, blk))

    return pl.pallas_call(
        functools.partial(_ssd_kernel, rev=True, nc=nc),
        grid=(bsz, nc),
        in_specs=[chunk(EV_XBC_W, 0), chunk(EV_DT_W, EV_DT_BLK), const((1, LANES)), const((1, LANES)),
                  const((LANES, D_MODEL)), chunk(EV_Z_W, EV_Z_BLK), chunk(D_MODEL, 0),
                  const((1, D_MODEL)), const((1, D_MODEL))],
        out_specs=chunk(D_MODEL, 0),
        out_shape=jax.ShapeDtypeStruct((bsz, t, D_MODEL), F32),
        scratch_shapes=[pltpu.VMEM((SSD_GROUPS, SSD_STATE, D_MODEL // SSD_GROUPS), F32)],
        compiler_params=_cparams("parallel", "arbitrary"),
        name="ssd_bwd",
    )(xa, u3, dt_bias, a_log, _head_expand_matrix(True), u3, yf, d_skip, norm_w)


def _conv_module_kernel(val_ref, gate_ref, vp_ref, gp_ref, vn_ref, gn_ref,
                        dw_ref, db_ref, lnw_ref, lnb_ref, o_ref, pad_ref, sh_ref, *, nt):
    t = pl.program_id(1)
    tt = val_ref.shape[1]
    h = CONV_HALO
    pad_ref[0:h, :] = vp_ref[0] * jax.nn.sigmoid(gp_ref[0]) * (t > 0).astype(F32)
    pad_ref[h:h + tt, :] = val_ref[0] * jax.nn.sigmoid(gate_ref[0])
    pad_ref[h + tt:2 * h + tt, :] = vn_ref[0] * jax.nn.sigmoid(gn_ref[0]) * (t < nt - 1).astype(F32)
    rows = tt + 2 * h - SUBLANES
    for b in range(1, SUBLANES):
        sh_ref[b - 1] = pad_ref[b:b + rows, :]
    acc = db_ref[...]
    for j in range(CONV_WIDTH):
        a, b = divmod(h - CONV_WIDTH // 2 + j, SUBLANES)
        lo = a * SUBLANES
        tap = pad_ref[lo:lo + tt, :] if b == 0 else sh_ref[b - 1, lo:lo + tt, :]
        acc = acc + dw_ref[j:j + 1, :] * tap
    xc = acc - jnp.mean(acc, axis=-1, keepdims=True)
    xn = xc * lax.rsqrt(jnp.mean(xc * xc, axis=-1, keepdims=True) + 1e-5)
    yv = xn * lnw_ref[...] + lnb_ref[...]
    o_ref[0] = yv * jax.nn.sigmoid(yv)


def _conv_module(u3, dw_w, dw_b, ln_w, ln_b, tt):
    bsz, t, _ = u3.shape
    nt = t // tt
    rh = tt // CONV_HALO
    lasth = t // CONV_HALO - 1
    d = D_MODEL

    def const(shape):
        return pl.BlockSpec(shape, lambda b, i: (0,) * len(shape))

    main = lambda blk: pl.BlockSpec((1, tt, d), lambda b, i: (b, i, blk))
    prev = lambda blk: pl.BlockSpec((1, CONV_HALO, d), lambda b, i: (b, jnp.maximum(i * rh - 1, 0), blk))
    nxt = lambda blk: pl.BlockSpec((1, CONV_HALO, d), lambda b, i: (b, jnp.minimum((i + 1) * rh, lasth), blk))
    return pl.pallas_call(
        functools.partial(_conv_module_kernel, nt=nt),
        grid=(bsz, nt),
        in_specs=[main(0), main(1), prev(0), prev(1), nxt(0), nxt(1),
                  const((4 * SUBLANES, d)), const((1, d)), const((1, d)), const((1, d))],
        out_specs=pl.BlockSpec((1, tt, d), lambda b, i: (b, i, 0)),
        out_shape=jax.ShapeDtypeStruct((bsz, t, d), F32),
        scratch_shapes=[pltpu.VMEM((tt + 2 * CONV_HALO, d), F32),
                        pltpu.VMEM((SUBLANES - 1, tt + 2 * CONV_HALO - SUBLANES, d), F32)],
        compiler_params=_cparams("parallel", "parallel"),
        name="conv_module",
    )(u3, u3, u3, u3, u3, u3, dw_w, dw_b, ln_w, ln_b)


def _attention_kernel(q_ref, k0_ref, k1_ref, k2_ref, v0_ref, v1_ref, v2_ref,
                      bias_ref, qw_ref, kw_ref, sink_ref, o_ref, *, nblk):
    i = pl.program_id(1)
    blk = ATT_BLOCK
    nqb = D_MODEL // LANES
    col = lax.broadcasted_iota(jnp.int32, (blk, 3 * blk), 1)
    oob = ((i == 0) & (col < blk)) | ((i == nblk - 1) & (col >= 2 * blk))
    lane_lo = lax.broadcasted_iota(jnp.int32, (blk, LANES), 1) < HEAD_DIM
    lane_lo3 = lax.broadcasted_iota(jnp.int32, (3 * blk, LANES), 1) < HEAD_DIM
    scale = HEAD_DIM ** -0.5

    def head_rms(x, w, lo):
        return x * lax.rsqrt(_head_sum(x * x, lo) * (1.0 / HEAD_DIM) + 1e-6) * w

    k_all = jnp.concatenate([k0_ref[0], k1_ref[0], k2_ref[0]], axis=0)
    v_all = jnp.concatenate([v0_ref[0], v1_ref[0], v2_ref[0]], axis=0)
    kn = [head_rms(k_all[:, kb * LANES:(kb + 1) * LANES], kw_ref[...], lane_lo3) for kb in range(2)]
    qn = [head_rms(q_ref[0, :, j * LANES:(j + 1) * LANES], qw_ref[...], lane_lo) for j in range(nqb)]
    half = nqb // 2
    logits = []
    for kb in range(2):
        stack = []
        for j in range(kb * half, (kb + 1) * half):
            q_lo = jnp.where(lane_lo, qn[j], 0.0)
            stack += [q_lo, qn[j] - q_lo]
        logits.append(_mm_nt(jnp.concatenate(stack, axis=0), kn[kb]))
    probs, rden = [], []
    for pos in range(N_HEADS):
        kb, idx = divmod(pos, 2 * half)
        lg = logits[kb][idx * blk:(idx + 1) * blk] * scale + bias_ref[pos]
        lg = jnp.where(oob, -jnp.inf, lg)
        sink = sink_ref[pos]
        m = jnp.maximum(jnp.max(lg, axis=-1, keepdims=True), sink)
        p = jnp.exp(lg - m)
        rden.append(1.0 / (jnp.sum(p, axis=-1, keepdims=True) + jnp.exp(sink - m)))
        probs.append(p.astype(BF16))
    for j in range(nqb):
        vb = v_all[:, (j // half) * LANES:(j // half + 1) * LANES]
        v_lo = jnp.where(lane_lo3, vb, 0.0)
        pv = _mm(jnp.concatenate([probs[2 * j], probs[2 * j + 1]], axis=1), jnp.concatenate([v_lo, vb - v_lo], axis=0))
        o_ref[0, :, j * LANES:(j + 1) * LANES] = pv * jnp.where(lane_lo, rden[2 * j], rden[2 * j + 1])


def _attention(u3, bias, q_norm_w, k_norm_w, sink):
    bsz, t, _ = u3.shape
    nblk = t // ATT_BLOCK
    d = D_MODEL
    kvw = ATT_KV_HEADS * HEAD_DIM
    kblk = (2 * D_MODEL + d) // kvw
    vblk = kblk + 1

    def kv_spec(blk, off):
        return pl.BlockSpec((1, ATT_BLOCK, kvw), lambda b, i: (b, jnp.clip(i + off, 0, nblk - 1), blk))

    def const(shape):
        return pl.BlockSpec(shape, lambda b, i: (0,) * len(shape))

    return pl.pallas_call(
        functools.partial(_attention_kernel, nblk=nblk),
        grid=(bsz, nblk),
        in_specs=[pl.BlockSpec((1, ATT_BLOCK, d), lambda b, i: (b, i, 2)),
                  kv_spec(kblk, -1), kv_spec(kblk, 0), kv_spec(kblk, 1),
                  kv_spec(vblk, -1), kv_spec(vblk, 0), kv_spec(vblk, 1),
                  const((N_HEADS, ATT_BLOCK, 3 * ATT_BLOCK)), const((1, LANES)), const((1, LANES)),
                  pl.BlockSpec(memory_space=pltpu.SMEM)],
        out_specs=pl.BlockSpec((1, ATT_BLOCK, d), lambda b, i: (b, i, 0)),
        out_shape=jax.ShapeDtypeStruct((bsz, t, d), F32),
        compiler_params=_cparams("parallel", "parallel"),
        name="attention",
    )(u3, u3, u3, u3, u3, u3, u3, bias, q_norm_w, k_norm_w, sink)


def _t5_bucket(rel):
    nb = REL_BUCKETS // 2
    max_exact = nb // 2
    n = jnp.abs(rel)
    nf = jnp.maximum(n, 1).astype(jnp.float32)
    large = max_exact + (jnp.log(nf / max_exact) / math.log(REL_MAX_DIST / max_exact)
                         * (nb - max_exact)).astype(jnp.int32)
    large = jnp.minimum(large, nb - 1)
    return (rel > 0).astype(jnp.int32) * nb + jnp.where(n < max_exact, n, large)


def _attention_bias(rel_bias):
    rel = jnp.arange(3 * ATT_BLOCK)[None, :] - ATT_BLOCK - jnp.arange(ATT_BLOCK)[:, None]
    bias = rel_bias.astype(F32)[_t5_bucket(rel)].transpose(2, 0, 1)
    bias = jnp.where((jnp.abs(rel) <= ATT_BLOCK)[None], bias, -jnp.inf)
    return bias[jnp.asarray(_ATT_HEAD_ORDER)]


_ATT_HEAD_ORDER = tuple(8 * (j // 4) + 4 * s + (j % 4) for j in range(8) for s in range(2))


def _permute_heads(w, axis):
    shape = w.shape
    split = shape[:axis] + (N_HEADS, HEAD_DIM) + shape[axis + 1:]
    return jnp.take(w.reshape(split), jnp.asarray(_ATT_HEAD_ORDER), axis=axis).reshape(shape)


def _pad_cols(w, width):
    return jnp.pad(w, ((0, 0), (0, width - w.shape[1])))


def _even_in_weight(w_in):
    z = w_in[:, :D_MODEL]
    xbc = w_in[:, D_MODEL:D_MODEL + SSD_XBC]
    dt = w_in[:, D_MODEL + SSD_XBC:D_MODEL + SSD_XBC + 2 * N_HEADS]
    p = w_in[:, D_MODEL + SSD_XBC + 2 * N_HEADS:]
    w = jnp.concatenate([p[:, :EV_RKV_W], xbc, p[:, EV_RKV_W:], _pad_cols(dt, 2 * LANES), z], axis=1)
    assert w.shape[1] == EV_W
    return w.astype(BF16)


def _row(v):
    return v.reshape(1, -1).astype(F32)


def _even_params(e, ev_w_in, ev_w_out, ssd_conv_w, ssd_conv_b, ssd_dt_bias, ssd_a_log, ssd_d, ssd_norm_w,
                 rwkv_mu, rwkv_w0, rwkv_w2, rwkv_a0, rwkv_a2, rwkv_g2, rwkv_k_k, rwkv_k_a, rwkv_r_k,
                 rwkv_ln_w, rwkv_ln_b):
    zeros = lambda n: jnp.zeros((n, D_MODEL), F32)
    lr_pad = lambda w, lo: jnp.concatenate([zeros(lo), w.astype(F32), zeros(EV_LR_W - lo - w.shape[0])], 0).astype(BF16)
    return dict(
        w_in=_even_in_weight(ev_w_in[e]),
        w_out=ev_w_out[e].astype(BF16),
        conv_w=jnp.pad(ssd_conv_w[e].astype(F32), ((0, SUBLANES - SSD_CONV), (0, 0))),
        conv_b=_row(ssd_conv_b[e]),
        dt_bias=_pad_cols(_row(ssd_dt_bias[e]), LANES),
        a_log=_pad_cols(_row(ssd_a_log[e]), LANES),
        d_skip=_row(jnp.repeat(ssd_d[e], HEAD_DIM)),
        ssd_norm_w=_row(ssd_norm_w[e]),
        mu_rkv=_row(rwkv_mu[e, :EV_RKV_W]),
        mu_lr=_row(rwkv_mu[e, EV_RKV_W:]),
        a0=_row(rwkv_a0[e]),
        w0=rwkv_w0[e].reshape(2, 1, D_MODEL).astype(F32),
        w2=jnp.stack([lr_pad(rwkv_w2[e, d], 0) for d in range(2)]),
        a2=lr_pad(rwkv_a2[e], 64),
        g2=lr_pad(rwkv_g2[e], 128),
        k_k=_row(rwkv_k_k[e]), k_a=_row(rwkv_k_a[e]), r_k=_row(rwkv_r_k[e]),
        ln_w=_row(rwkv_ln_w[e]), ln_b=_row(rwkv_ln_b[e]),
    )


def _odd_params(w_in, w_out, dw_w, dw_b, ln_w, ln_b, q_norm_w, k_norm_w, sink):
    q_lo, q_hi = 2 * D_MODEL, 3 * D_MODEL
    w_in = jnp.concatenate([w_in[:, :q_lo], _permute_heads(w_in[:, q_lo:q_hi], 1), w_in[:, q_hi:]], axis=1)
    w_out = jnp.concatenate([w_out[:D_MODEL], _permute_heads(w_out[D_MODEL:], 0)], axis=0)
    return dict(
        w_in=w_in.astype(BF16), w_out=w_out.astype(BF16),
        dw_w=jnp.pad(dw_w.astype(F32), ((0, 4 * SUBLANES - CONV_WIDTH), (0, 0))),
        dw_b=_row(dw_b), ln_w=_row(ln_w), ln_b=_row(ln_b),
        q_w=_row(jnp.tile(q_norm_w, 2)), k_w=_row(jnp.tile(k_norm_w, 2)),
        sink=sink.astype(F32)[jnp.asarray(_ATT_HEAD_ORDER)],
    )


def _even_layer(x3, norm_w, p, tm):
    bsz, t, d = x3.shape
    x2 = x3.reshape(bsz * t, d)
    u3 = _norm_matmul(x2, norm_w, p["w_in"], min(bsz * t, NORM_MM_ROWS), 512).reshape(bsz, t, EV_W)
    ya_f, xa = _ssd_fwd(u3, p["conv_w"], p["conv_b"], p["dt_bias"], p["a_log"])
    ya = _ssd_bwd(u3, xa, ya_f, p["dt_bias"], p["a_log"], p["d_skip"], p["ssd_norm_w"])
    rkvs, a, lw0, lw1, g = _rwkv_prep(u3, p["mu_rkv"], p["mu_lr"], p["a0"], p["a2"], p["w0"], p["w2"], p["g2"],
                                      min(t, 256))
    yb_f = _rwkv_scan(rkvs, a, lw0, p["k_k"], p["k_a"], rev=False)
    yb = _rwkv_scan(rkvs, a, lw1, p["k_k"], p["k_a"], rev=True,
                    final_args=(yb_f, g, p["r_k"], p["ln_w"], p["ln_b"]))
    out = _proj_res(x2, ya.reshape(bsz * t, d), yb.reshape(bsz * t, d), p["w_out"], tm)
    return out.reshape(bsz, t, d)


def _odd_layer(x3, norm_w, p, bias, tm):
    bsz, t, d = x3.shape
    x2 = x3.reshape(bsz * t, d)
    u3 = _norm_matmul(x2, norm_w, p["w_in"], min(bsz * t, NORM_MM_ROWS), 512).reshape(bsz, t, -1)
    yc = _conv_module(u3, p["dw_w"], p["dw_b"], p["ln_w"], p["ln_b"], min(t, 256))
    yd = _attention(u3, bias, p["q_w"], p["k_w"], p["sink"])
    out = _proj_res(x2, yc.reshape(bsz * t, d), yd.reshape(bsz * t, d), p["w_out"], tm)
    return out.reshape(bsz, t, d)


def _trunk(x3, layers, bias):
    bsz, t, d = x3.shape
    tm = min(bsz * t, 512)
    for kind, norm_mix, p, norm_ffn, ffn_wi, ffn_wo in layers:
        if kind == "even":
            x3 = _even_layer(x3, norm_mix, p, tm)
        else:
            x3 = _odd_layer(x3, norm_mix, p, bias, tm)
        x3 = _ffn(x3.reshape(bsz * t, d), norm_ffn, ffn_wi, ffn_wo, tm).reshape(bsz, t, d)
    return x3


def kernel(x_prompt, x_sample, rel_bias, norm_mix_w, norm_ffn_w, ffn_w_in, ffn_w_out, ev_w_in, ev_w_out, ssd_conv_w, ssd_conv_b, ssd_dt_bias, ssd_a_log, ssd_d, ssd_norm_w, rwkv_mu, rwkv_w0, rwkv_w2, rwkv_a0, rwkv_a2, rwkv_g2, rwkv_k_k, rwkv_k_a, rwkv_r_k, rwkv_ln_w, rwkv_ln_b, od_w_in, od_w_out, conv_dw_w, conv_dw_b, conv_ln_w, conv_ln_b, att_q_norm_w, att_k_norm_w, att_sink):
    depth = norm_mix_w.shape[0]
    bias = _attention_bias(rel_bias)
    layers = []
    for layer in range(depth):
        i = layer // 2
        if layer % 2 == 0:
            kind = "even"
            p = _even_params(i, ev_w_in, ev_w_out, ssd_conv_w, ssd_conv_b, ssd_dt_bias, ssd_a_log, ssd_d,
                             ssd_norm_w, rwkv_mu, rwkv_w0, rwkv_w2, rwkv_a0, rwkv_a2, rwkv_g2, rwkv_k_k,
                             rwkv_k_a, rwkv_r_k, rwkv_ln_w, rwkv_ln_b)
        else:
            kind = "odd"
            p = _odd_params(od_w_in[i], od_w_out[i], conv_dw_w[i], conv_dw_b[i], conv_ln_w[i], conv_ln_b[i],
                            att_q_norm_w[i], att_k_norm_w[i], att_sink[i])
        layers.append((kind, _row(norm_mix_w[layer]), p, _row(norm_ffn_w[layer]),
                       ffn_w_in[layer].astype(BF16), ffn_w_out[layer].astype(BF16)))
    return (_trunk(x_prompt, layers, bias), _trunk(x_sample, layers, bias))
```

```python
import functools
import math

import jax
import jax.numpy as jnp
import numpy as np
from jax import lax
from jax.experimental import pallas as pl
from jax.experimental.pallas import tpu as pltpu

F32 = jnp.float32
BF16 = jnp.bfloat16
HIGHEST = lax.Precision.HIGHEST

LANES = 128
SUBLANES = 8
VMEM_LIMIT_BYTES = 56 * 1024 * 1024

D_MODEL = 1024
HEAD_DIM = 64
D_FF = 2816
FF_CHUNK = 256
NORM_MM_ROWS = 2048
N_HEADS = 16
SSD_GROUPS = 2
SSD_STATE = 128
SSD_CONV = 5
SSD_XBC = D_MODEL + 2 * SSD_GROUPS * SSD_STATE
CHUNK = 128
RWKV_LN_EPS = 64e-5
CONV_WIDTH = 31
CONV_HALO = 16
ATT_BLOCK = 128
ATT_KV_HEADS = 4
ATT_GQ = 4
REL_BUCKETS = 32
REL_MAX_DIST = 128

EV_W = 6144
EV_RKV_W, EV_XBC_W, EV_LR_W, EV_DT_W, EV_Z_W = 3072, SSD_XBC, 256, LANES, D_MODEL
EV_XBC_BLK = 3072 // EV_XBC_W
EV_LR_BLK = 4608 // EV_LR_W
EV_DT_BLK = 4864 // EV_DT_W
EV_Z_BLK = 5120 // EV_Z_W


def _cparams(*sem):
    return pltpu.CompilerParams(dimension_semantics=sem, vmem_limit_bytes=VMEM_LIMIT_BYTES)


def _mm(a, b):
    return jnp.dot(a.astype(BF16), b.astype(BF16), preferred_element_type=F32)


def _mm_nt(a, b):
    return lax.dot_general(a.astype(BF16), b.astype(BF16), (((1,), (1,)), ((), ())),
                           preferred_element_type=F32)


def _mm_tn(a, b):
    return lax.dot_general(a.astype(BF16), b.astype(BF16), (((0,), (0,)), ((), ())),
                           preferred_element_type=F32)


def _rms(x, w, eps=1e-6):
    return x * lax.rsqrt(jnp.mean(x * x, axis=-1, keepdims=True) + eps) * w


def _norm_matmul_kernel(x_ref, nw_ref, w_ref, o_ref, h_ref):
    @pl.when(pl.program_id(1) == 0)
    def _():
        h_ref[...] = _rms(x_ref[...], nw_ref[...]).astype(BF16)

    o_ref[...] = jnp.dot(h_ref[...], w_ref[...], preferred_element_type=F32)


def _norm_matmul(x2, nw, w, tm, tn):
    n, d = x2.shape
    nout = w.shape[1]
    return pl.pallas_call(
        _norm_matmul_kernel,
        grid=(n // tm, nout // tn),
        in_specs=[pl.BlockSpec((tm, d), lambda i, j: (i, 0)),
                  pl.BlockSpec((1, d), lambda i, j: (0, 0)),
                  pl.BlockSpec((d, tn), lambda i, j: (0, j))],
        out_specs=pl.BlockSpec((tm, tn), lambda i, j: (i, j)),
        out_shape=jax.ShapeDtypeStruct((n, nout), F32),
        scratch_shapes=[pltpu.VMEM((tm, d), BF16)],
        compiler_params=_cparams("parallel", "arbitrary"),
        name="norm_matmul",
    )(x2, nw, w)


def _proj_ffn_kernel(x_ref, a_ref, b_ref, wm_ref, nw_ref, wi_ref, wo_ref, o_ref):
    d = x_ref.shape[1]
    x = x_ref[...] + _mm(a_ref[...], wm_ref[0:d, :]) + _mm(b_ref[...], wm_ref[d:2 * d, :])
    h = _rms(x, nw_ref[...]).astype(BF16)
    o_ref[...] = x
    for c in range(D_FF // FF_CHUNK):
        lo = c * FF_CHUNK
        g = jnp.dot(h, wi_ref[:, lo:lo + FF_CHUNK], preferred_element_type=F32)
        u = jnp.dot(h, wi_ref[:, D_FF + lo:D_FF + lo + FF_CHUNK], preferred_element_type=F32)
        act = (g * jax.nn.sigmoid(g) * u).astype(BF16)
        o_ref[...] += jnp.dot(act, wo_ref[lo:lo + FF_CHUNK, :], preferred_element_type=F32)


def _proj_ffn(x2, a2, b2, w_mix, nw, wi, wo, tm):
    n, d = x2.shape
    row = pl.BlockSpec((tm, d), lambda i: (i, 0))
    resident = lambda shape: pl.BlockSpec(shape, lambda i: (0, 0), pipeline_mode=pl.Buffered(1))
    return pl.pallas_call(
        _proj_ffn_kernel,
        grid=(n // tm,),
        in_specs=[row, row, row, resident((2 * d, d)), pl.BlockSpec((1, d), lambda i: (0, 0)),
                  resident((d, 2 * D_FF)), resident((D_FF, d))],
        out_specs=row,
        out_shape=jax.ShapeDtypeStruct((n, d), F32),
        compiler_params=_cparams("parallel"),
        name="proj_ffn",
    )(x2, a2, b2, w_mix, nw, wi, wo)


def _shift_mix(p, prev_row, next_row, mu):
    tt = p.shape[0]
    row8 = lax.broadcasted_iota(jnp.int32, (SUBLANES, p.shape[1]), 0)
    prev = pltpu.roll(p, 1, 0)
    prev = jnp.concatenate([jnp.where(row8 == 0, prev_row, prev[:SUBLANES]), prev[SUBLANES:]], axis=0)
    nxt = pltpu.roll(p, tt - 1, 0)
    nxt = jnp.concatenate([nxt[:tt - SUBLANES], jnp.where(row8 == SUBLANES - 1, next_row, nxt[tt - SUBLANES:])],
                          axis=0)
    return p + mu * (0.5 * (prev + nxt) - p)


def _rwkv_prep_kernel(lr_ref, lr_p_ref, lr_n_ref, mu_lr_ref, a0_ref, a2_ref, w0_ref, w2_ref, g2_ref,
                      a_ref, lw0_ref, lw1_ref, g_ref, *, nt):
    t = pl.program_id(1)
    keep_p = (t > 0).astype(F32)
    keep_n = (t < nt - 1).astype(F32)
    lr = _shift_mix(lr_ref[0], lr_p_ref[0, 7:8, :] * keep_p, lr_n_ref[0, 0:1, :] * keep_n, mu_lr_ref[...])
    a_ref[0] = jax.nn.sigmoid(a0_ref[...] + _mm(lr, a2_ref[...]))
    g_ref[0] = _mm(jax.nn.sigmoid(lr), g2_ref[...])
    th = jnp.tanh(lr)
    for d, out in enumerate((lw0_ref, lw1_ref)):
        wlog = -jax.nn.softplus(-(w0_ref[d] + _mm(th, w2_ref[d]))) - 0.5
        out[0] = -jnp.exp(wlog)


def _rwkv_prep(u3, mu_lr, a0, a2p, w0, w2p, g2p, tt):
    bsz, t, _ = u3.shape
    nt = t // tt
    r8 = tt // SUBLANES
    last8 = t // SUBLANES - 1
    d = D_MODEL

    def const(shape):
        return pl.BlockSpec(shape, lambda b, i: (0,) * len(shape))

    out = pl.BlockSpec((1, tt, d), lambda b, i: (b, i, 0))
    sds = jax.ShapeDtypeStruct((bsz, t, d), F32)
    return pl.pallas_call(
        functools.partial(_rwkv_prep_kernel, nt=nt),
        grid=(bsz, nt),
        in_specs=[pl.BlockSpec((1, tt, EV_LR_W), lambda b, i: (b, i, EV_LR_BLK)),
                  pl.BlockSpec((1, SUBLANES, EV_LR_W), lambda b, i: (b, jnp.maximum(i * r8 - 1, 0), EV_LR_BLK)),
                  pl.BlockSpec((1, SUBLANES, EV_LR_W), lambda b, i: (b, jnp.minimum((i + 1) * r8, last8), EV_LR_BLK)),
                  const((1, EV_LR_W)), const((1, d)), const((EV_LR_W, d)),
                  const((2, 1, d)), const((2, EV_LR_W, d)), const((EV_LR_W, d))],
        out_specs=[out, out, out, out],
        out_shape=[sds, sds, sds, sds],
        compiler_params=_cparams("parallel", "parallel"),
        name="rwkv_prep",
    )(u3, u3, u3, mu_lr, a0, a2p, w0, w2p, g2p)


def _head_sum(x, lane_lo):
    s0 = jnp.sum(jnp.where(lane_lo, x, 0.0), axis=-1, keepdims=True)
    s1 = jnp.sum(jnp.where(lane_lo, 0.0, x), axis=-1, keepdims=True)
    return jnp.where(lane_lo, s0, s1)


def _same_block(ti, si, log2_size):
    return jnp.right_shift(ti, log2_size) == jnp.right_shift(si, log2_size)


def _unit_lower_inverses(a_list, ti, si, eye):
    sh = 4
    x = [jnp.where(_same_block(ti, si, sh), a, 0.0) for a in a_list]
    t_inv = [eye + xi for xi in x]
    for _ in range(sh - 1):
        x = [_mm(xi, xi) for xi in x]
        t_inv = [t + _mm(t, xi) for t, xi in zip(t_inv, x)]
    while (1 << sh) < a_list[0].shape[0]:
        off = jnp.logical_and(jnp.logical_not(_same_block(ti, si, sh)), _same_block(ti, si, sh + 1))
        at = [_mm(jnp.where(off, a, 0.0), t) for a, t in zip(a_list, t_inv)]
        t_inv = [t + _mm(t, m) for t, m in zip(t_inv, at)]
        sh += 1
    return t_inv


def _cumsum_rows(tri_bf16, x):
    hi = x.astype(BF16)
    lo = (x - hi.astype(F32)).astype(BF16)
    return (jnp.dot(tri_bf16, hi, preferred_element_type=F32)
            + jnp.dot(tri_bf16, lo, preferred_element_type=F32))


def _rwkv_scan_kernel(*refs, rev, nc, final, pairs):
    if final:
        (r_ref, k_ref, v_ref, a_ref, lw_ref, kk_w_ref, ka_w_ref,
         yf_ref, g_ref, rk_w_ref, lnw_ref, lnb_ref, o_ref, s_ref) = refs
    else:
        (r_ref, k_ref, v_ref, rp_ref, kp_ref, vp_ref, rn_ref, kn_ref, vn_ref, mu_ref,
         a_ref, lw_ref, kk_w_ref, ka_w_ref, o_ref, rkvs_ref, s_ref) = refs
    L = CHUNK
    c_id = pl.program_id(1)

    @pl.when(c_id == 0)
    def _():
        s_ref[...] = jnp.zeros_like(s_ref)

    if final:
        r_all, k_all, v_all = r_ref[0], k_ref[0], v_ref[0]
    else:
        cc = nc - 1 - c_id if rev else c_id
        keep_p = (cc > 0).astype(F32)
        keep_n = (cc < nc - 1).astype(F32)
        mixed = []
        for i, (m_ref, p_ref, n_ref) in enumerate(((r_ref, rp_ref, rn_ref), (k_ref, kp_ref, kn_ref),
                                                   (v_ref, vp_ref, vn_ref))):
            x = _shift_mix(m_ref[0], p_ref[0, SUBLANES - 1:SUBLANES, :] * keep_p, n_ref[0, 0:1, :] * keep_n,
                           mu_ref[:, i * D_MODEL:(i + 1) * D_MODEL])
            rkvs_ref[0, :, i * D_MODEL:(i + 1) * D_MODEL] = x
            mixed.append(x)
        r_all, k_all, v_all = mixed

    lane_lo = lax.broadcasted_iota(jnp.int32, (L, LANES), 1) < HEAD_DIM
    ti = lax.broadcasted_iota(jnp.int32, (L, L), 0)
    si = lax.broadcasted_iota(jnp.int32, (L, L), 1)
    tri = (si >= ti) if rev else (si <= ti)
    strict = (si > ti) if rev else (si < ti)
    eye = (si == ti).astype(F32)
    heads = [(q, h) for q in range(pairs) for h in range(2)]
    cols = [slice(q * LANES, (q + 1) * LANES) for q in range(pairs)]
    own = lambda h, x: jnp.where(lane_lo, x, 0.0) if h == 0 else jnp.where(lane_lo, 0.0, x)

    lw_all = lw_ref[0]
    cum_all = _cumsum_rows(jnp.where(tri, 1.0, 0.0).astype(BF16), lw_all)
    r = [r_all[:, c] for c in cols]
    v = [v_all[:, c] for c in cols]
    kmod, a_t, b_t, k_t, r_t, b_h, k_h, tot = [], [], [], [], [], [], [], []
    for q, c in enumerate(cols):
        k, a, cum, lw = k_all[:, c], a_ref[0, :, c], cum_all[:, c], lw_all[:, c]
        kk = k * kk_w_ref[:, c]
        kk = kk / jnp.maximum(jnp.sqrt(_head_sum(kk * kk, lane_lo)), 1e-12)
        km = k * (1.0 + (a - 1.0) * ka_w_ref[:, c])
        ib = kk * a
        tq = cum[0:1, :] if rev else cum[L - 1:L, :]
        p_inv = jnp.exp(-cum)
        p_end = jnp.exp(tq - cum)
        kmod.append(km)
        tot.append(tq)
        a_t.append(-kk * jnp.exp(cum - lw))
        b_t.append(ib * p_inv)
        k_t.append(km * p_inv)
        r_t.append(r[q] * jnp.exp(cum))
        b_h.append(ib * p_end)
        k_h.append(km * p_end)

    a_m = [own(h, a_t[q]) for q, h in heads]
    gram = [_mm_nt(jnp.concatenate([a_m[i], own(h, r_t[q])], axis=0),
                   jnp.concatenate([b_t[q], k_t[q]], axis=0)) for i, (q, h) in enumerate(heads)]
    a_ab = [jnp.where(strict, g[:L, :L], 0.0) for g in gram]
    av = [_mm(jnp.where(strict, g[:L, L:], 0.0), own(h, v[q])) for g, (q, h) in zip(gram, heads)]
    m_rb = [jnp.where(tri, g[L:, :L], 0.0) for g in gram]
    m_rk = [jnp.where(tri, g[L:, L:], 0.0) for g in gram]
    t_inv = _unit_lower_inverses(a_ab, ti, si, eye)

    w12 = [_mm(jnp.concatenate([t_inv[2 * q], t_inv[2 * q + 1]], axis=1),
               jnp.concatenate([jnp.concatenate([a_m[2 * q], av[2 * q]], axis=1),
                                jnp.concatenate([a_m[2 * q + 1], av[2 * q + 1]], axis=1)], axis=0))
           for q in range(pairs)]
    s0 = [s_ref[q] for q in range(pairs)]
    u = [_mm_nt(w12[q][:, :LANES], s0[q]) + w12[q][:, LANES:] for q in range(pairs)]
    y = []
    for q in range(pairs):
        u_lo, v_lo = own(0, u[q]), own(0, v[q])
        uv = jnp.concatenate([u_lo, u[q] - u_lo, v_lo, v[q] - v_lo], axis=0)
        m_cat = jnp.concatenate([m_rb[2 * q], m_rb[2 * q + 1], m_rk[2 * q], m_rk[2 * q + 1]], axis=1)
        y.append(_mm_nt(r_t[q], s0[q]) + _mm(m_cat, uv))
    for q in range(pairs):
        s_new = s0[q] * jnp.exp(tot[q]) + _mm_tn(jnp.concatenate([u[q], v[q]], axis=0),
                                                 jnp.concatenate([b_h[q], k_h[q]], axis=0))
        s_ref[q] = jnp.where(_same_block(ti, si, 6), s_new, 0.0)

    for q, c in enumerate(cols):
        if not final:
            o_ref[0, :, c] = y[q]
        else:
            yq = y[q] + yf_ref[0, :, c]
            inv_n = 1.0 / HEAD_DIM
            yc = yq - _head_sum(yq, lane_lo) * inv_n
            yn = yc * lax.rsqrt(_head_sum(yc * yc, lane_lo) * inv_n + RWKV_LN_EPS)
            bonus = _head_sum(r[q] * kmod[q] * rk_w_ref[:, c], lane_lo)
            o_ref[0, :, c] = (yn * lnw_ref[:, c] + lnb_ref[:, c] + bonus * v[q]) * g_ref[0, :, c]


def _rwkv_scan_fwd(u3, mu_rkv, a, lw, kk_w, ka_w):
    bsz, t, _ = u3.shape
    nc = t // CHUNK
    d = D_MODEL
    r8 = CHUNK // SUBLANES
    last8 = t // SUBLANES - 1
    blk = lambda j: pl.BlockSpec((1, CHUNK, d), lambda b, c: (b, c, j))
    prev = lambda j: pl.BlockSpec((1, SUBLANES, d), lambda b, c: (b, jnp.maximum(c * r8 - 1, 0), j))
    nxt = lambda j: pl.BlockSpec((1, SUBLANES, d), lambda b, c: (b, jnp.minimum((c + 1) * r8, last8), j))
    par = lambda w: pl.BlockSpec((1, w), lambda b, c: (0, 0))
    return pl.pallas_call(
        functools.partial(_rwkv_scan_kernel, rev=False, nc=nc, final=False, pairs=d // LANES),
        grid=(bsz, nc),
        in_specs=[blk(0), blk(1), blk(2), prev(0), prev(1), prev(2), nxt(0), nxt(1), nxt(2), par(EV_RKV_W),
                  blk(0), blk(0), par(d), par(d)],
        out_specs=[blk(0), pl.BlockSpec((1, CHUNK, EV_RKV_W), lambda b, c: (b, c, 0))],
        out_shape=[jax.ShapeDtypeStruct((bsz, t, d), F32), jax.ShapeDtypeStruct((bsz, t, EV_RKV_W), F32)],
        scratch_shapes=[pltpu.VMEM((d // LANES, LANES, LANES), F32)],
        compiler_params=_cparams("parallel", "arbitrary"),
        name="rwkv_scan_fwd",
    )(u3, u3, u3, u3, u3, u3, u3, u3, u3, mu_rkv, a, lw, kk_w, ka_w)


def _rwkv_scan_bwd(rkvs, a, lw, kk_w, ka_w, yf, g, rk_w, ln_w, ln_b):
    bsz, t, _ = rkvs.shape
    nc = t // CHUNK
    d = D_MODEL
    blk = lambda j: pl.BlockSpec((1, CHUNK, d), lambda b, c: (b, nc - 1 - c, j))
    par = pl.BlockSpec((1, d), lambda b, c: (0, 0))
    return pl.pallas_call(
        functools.partial(_rwkv_scan_kernel, rev=True, nc=nc, final=True, pairs=d // LANES),
        grid=(bsz, nc),
        in_specs=[blk(0), blk(1), blk(2), blk(0), blk(0), par, par, blk(0), blk(0), par, par, par],
        out_specs=blk(0),
        out_shape=jax.ShapeDtypeStruct((bsz, t, d), F32),
        scratch_shapes=[pltpu.VMEM((d // LANES, LANES, LANES), F32)],
        compiler_params=_cparams("parallel", "arbitrary"),
        name="rwkv_scan_bwd",
    )(rkvs, rkvs, rkvs, a, lw, kk_w, ka_w, yf, g, rk_w, ln_w, ln_b)


def _split_dot(x, w_bf16, terms, lhs=False):
    out, rem = None, x
    for _ in range(terms):
        piece = rem.astype(BF16)
        rem = rem - piece.astype(F32)
        part = (jnp.dot(w_bf16, piece, preferred_element_type=F32) if lhs
                else jnp.dot(piece, w_bf16, preferred_element_type=F32))
        out = part if out is None else out + part
    return out


def _ssd_kernel(*refs, rev, nc):
    if rev:
        (xa_ref, dt_ref, dtb_ref, alog_ref, ex_ref, z_ref, yf_ref, dsk_ref, nw_ref, o_ref, h_ref) = refs
    else:
        (xbc_ref, xp_ref, xn_ref, dt_ref, cw_ref, cb_ref, dtb_ref, alog_ref, ex_ref,
         o_ref, xa_out_ref, h_ref, pad_ref) = refs
    L = CHUNK
    c = pl.program_id(1)
    cc = nc - 1 - c if rev else c

    @pl.when(c == 0)
    def _():
        h_ref[...] = jnp.zeros_like(h_ref)

    if rev:
        xa = xa_ref[0]
    else:
        pad_ref[0:SUBLANES, :] = xp_ref[0] * (cc > 0).astype(F32)
        pad_ref[SUBLANES:SUBLANES + L, :] = xbc_ref[0]
        pad_ref[SUBLANES + L:2 * SUBLANES + L, :] = xn_ref[0] * (cc < nc - 1).astype(F32)
        acc = cb_ref[...] + cw_ref[0:1, :] * pad_ref[SUBLANES - 2:SUBLANES - 2 + L, :]
        for j in range(1, SSD_CONV):
            lo = SUBLANES - 2 + j
            acc = acc + cw_ref[j:j + 1, :] * pad_ref[lo:lo + L, :]
        xa = acc * jax.nn.sigmoid(acc)
        xa_out_ref[0] = xa
    xs = xa[:, :D_MODEL]
    n_bc = SSD_GROUPS * SSD_STATE
    gw = D_MODEL // SSD_GROUPS

    dtv = jax.nn.softplus(dt_ref[0] + dtb_ref[...])
    da = dtv * -jnp.exp(alog_ref[...])
    ti = lax.broadcasted_iota(jnp.int32, (L, L), 0)
    si = lax.broadcasted_iota(jnp.int32, (L, L), 1)
    tri = (si >= ti) if rev else (si <= ti)
    cs = _split_dot(da, jnp.where(tri, 1.0, 0.0).astype(BF16), 3, lhs=True)
    cs_t = cs.T
    edge = 0 if rev else L - 1
    ex = ex_ref[...]
    dt_e = _split_dot(dtv, ex, 2)
    cs_e = _split_dot(cs, ex, 3)
    tot_e = cs_e[edge:edge + 1, :]
    xd = xs * dt_e
    xst = xd * jnp.exp(tot_e - cs_e)
    ecs = jnp.exp(cs_e)

    lane_lo = lax.broadcasted_iota(jnp.int32, (L, LANES), 1) < HEAD_DIM
    bm = [xa[:, D_MODEL + g * SSD_STATE:D_MODEL + (g + 1) * SSD_STATE] for g in range(SSD_GROUPS)]
    cm = [xa[:, D_MODEL + n_bc + g * SSD_STATE:D_MODEL + n_bc + (g + 1) * SSD_STATE] for g in range(SSD_GROUPS)]
    cb = [_mm_nt(cm[g], bm[g]) for g in range(SSD_GROUPS)]
    h0 = [h_ref[g] for g in range(SSD_GROUPS)]
    y_off = [_mm(cm[g], h0[g]) for g in range(SSD_GROUPS)]
    st = [_mm(bm[g].T, xst[:, g * gw:(g + 1) * gw]) for g in range(SSD_GROUPS)]
    for g in range(SSD_GROUPS):
        h_ref[g] = h0[g] * jnp.exp(tot_e[:, g * gw:(g + 1) * gw]) + st[g]

    off = N_HEADS if rev else 0
    gmat = []
    for hh in range(N_HEADS):
        col = cs[:, off + hh:off + hh + 1]
        row = cs_t[off + hh:off + hh + 1, :]
        decay = jnp.exp(jnp.where(tri, col - row, -jnp.inf))
        gmat.append((cb[hh // (N_HEADS // SSD_GROUPS)] * decay).astype(BF16))
    y_diag = []
    for p in range(N_HEADS // 2):
        xp = xd[:, p * LANES:(p + 1) * LANES]
        x_lo = jnp.where(lane_lo, xp, 0.0)
        rhs = jnp.concatenate([x_lo, xp - x_lo], axis=0)
        y_diag.append(_mm(jnp.concatenate([gmat[2 * p], gmat[2 * p + 1]], axis=1), rhs))
    y = jnp.concatenate(y_diag, axis=1) + jnp.concatenate(y_off, axis=1) * ecs

    if not rev:
        o_ref[0] = y
    else:
        z = z_ref[0]
        y = (y + yf_ref[0] + dsk_ref[...] * xs) * (z * jax.nn.sigmoid(z))
        parts = []
        for g in range(SSD_GROUPS):
            yg = y[:, g * gw:(g + 1) * gw]
            parts.append(yg * lax.rsqrt(jnp.mean(yg * yg, axis=-1, keepdims=True) + 1e-6))
        o_ref[0] = jnp.concatenate(parts, axis=1) * nw_ref[...]


def _head_expand_matrix(rev):
    e = np.zeros((LANES, D_MODEL), np.float32)
    for hh in range(N_HEADS):
        e[(N_HEADS if rev else 0) + hh, hh * HEAD_DIM:(hh + 1) * HEAD_DIM] = 1.0
    return jnp.asarray(e, dtype=BF16)


def _ssd_fwd(u3, conv_w, conv_b, dt_bias, a_log):
    bsz, t, _ = u3.shape
    nc = t // CHUNK
    r8 = CHUNK // SUBLANES
    last8 = t // SUBLANES - 1

    def const(shape):
        return pl.BlockSpec(shape, lambda b, c: (0,) * len(shape))

    return pl.pallas_call(
        functools.partial(_ssd_kernel, rev=False, nc=nc),
        grid=(bsz, nc),
        in_specs=[
            pl.BlockSpec((1, CHUNK, EV_XBC_W), lambda b, c: (b, c, EV_XBC_BLK)),
            pl.BlockSpec((1, SUBLANES, EV_XBC_W), lambda b, c: (b, jnp.maximum(c * r8 - 1, 0), EV_XBC_BLK)),
            pl.BlockSpec((1, SUBLANES, EV_XBC_W), lambda b, c: (b, jnp.minimum((c + 1) * r8, last8), EV_XBC_BLK)),
            pl.BlockSpec((1, CHUNK, EV_DT_W), lambda b, c: (b, c, EV_DT_BLK)),
            const((SUBLANES, EV_XBC_W)), const((1, EV_XBC_W)), const((1, LANES)), const((1, LANES)),
            const((LANES, D_MODEL)),
        ],
        out_specs=[pl.BlockSpec((1, CHUNK, D_MODEL), lambda b, c: (b, c, 0)),
                   pl.BlockSpec((1, CHUNK, EV_XBC_W), lambda b, c: (b, c, 0))],
        out_shape=[jax.ShapeDtypeStruct((bsz, t, D_MODEL), F32), jax.ShapeDtypeStruct((bsz, t, EV_XBC_W), F32)],
        scratch_shapes=[pltpu.VMEM((SSD_GROUPS, SSD_STATE, D_MODEL // SSD_GROUPS), F32),
                        pltpu.VMEM((CHUNK + 2 * SUBLANES, EV_XBC_W), F32)],
        compiler_params=_cparams("parallel", "arbitrary"),
        name="ssd_fwd",
    )(u3, u3, u3, u3, conv_w, conv_b, dt_bias, a_log, _head_expand_matrix(False))


def _ssd_bwd(u3, xa, yf, dt_bias, a_log, d_skip, norm_w):
    bsz, t, _ = u3.shape
    nc = t // CHUNK

    def const(shape):
        return pl.BlockSpec(shape, lambda b, c: (0,) * len(shape))

    def chunk(w, blk):
        return pl.BlockSpec((1, CHUNK, w), lambda b, c: (b, nc - 1 - c, blk))

    return pl.pallas_call(
        functools.partial(_ssd_kernel, rev=True, nc=nc),
        grid=(bsz, nc),
        in_specs=[chunk(EV_XBC_W, 0), chunk(EV_DT_W, EV_DT_BLK), const((1, LANES)), const((1, LANES)),
                  const((LANES, D_MODEL)), chunk(EV_Z_W, EV_Z_BLK), chunk(D_MODEL, 0),
                  const((1, D_MODEL)), const((1, D_MODEL))],
        out_specs=chunk(D_MODEL, 0),
        out_shape=jax.ShapeDtypeStruct((bsz, t, D_MODEL), F32),
        scratch_shapes=[pltpu.VMEM((SSD_GROUPS, SSD_STATE, D_MODEL // SSD_GROUPS), F32)],
        compiler_params=_cparams("parallel", "arbitrary"),
        name="ssd_bwd",
    )(xa, u3, dt_bias, a_log, _head_expand_matrix(True), u3, yf, d_skip, norm_w)


def _conv_module_kernel(val_ref, gate_ref, vp_ref, gp_ref, vn_ref, gn_ref,
                        dw_ref, db_ref, lnw_ref, lnb_ref, o_ref, pad_ref, sh_ref, *, nt):
    t = pl.program_id(1)
    tt = val_ref.shape[1]
    h = CONV_HALO
    pad_ref[0:h, :] = vp_ref[0] * jax.nn.sigmoid(gp_ref[0]) * (t > 0).astype(F32)
    pad_ref[h:h + tt, :] = val_ref[0] * jax.nn.sigmoid(gate_ref[0])
    pad_ref[h + tt:2 * h + tt, :] = vn_ref[0] * jax.nn.sigmoid(gn_ref[0]) * (t < nt - 1).astype(F32)
    rows = tt + 2 * h - SUBLANES
    for b in range(1, SUBLANES):
        sh_ref[b - 1] = pad_ref[b:b + rows, :]
    acc = db_ref[...]
    for j in range(CONV_WIDTH):
        a, b = divmod(h - CONV_WIDTH // 2 + j, SUBLANES)
        lo = a * SUBLANES
        tap = pad_ref[lo:lo + tt, :] if b == 0 else sh_ref[b - 1, lo:lo + tt, :]
        acc = acc + dw_ref[j:j + 1, :] * tap
    xc = acc - jnp.mean(acc, axis=-1, keepdims=True)
    xn = xc * lax.rsqrt(jnp.mean(xc * xc, axis=-1, keepdims=True) + 1e-5)
    yv = xn * lnw_ref[...] + lnb_ref[...]
    o_ref[0] = yv * jax.nn.sigmoid(yv)


def _conv_module(u3, dw_w, dw_b, ln_w, ln_b, tt):
    bsz, t, _ = u3.shape
    nt = t // tt
    rh = tt // CONV_HALO
    lasth = t // CONV_HALO - 1
    d = D_MODEL

    def const(shape):
        return pl.BlockSpec(shape, lambda b, i: (0,) * len(shape))

    main = lambda blk: pl.BlockSpec((1, tt, d), lambda b, i: (b, i, blk))
    prev = lambda blk: pl.BlockSpec((1, CONV_HALO, d), lambda b, i: (b, jnp.maximum(i * rh - 1, 0), blk))
    nxt = lambda blk: pl.BlockSpec((1, CONV_HALO, d), lambda b, i: (b, jnp.minimum((i + 1) * rh, lasth), blk))
    return pl.pallas_call(
        functools.partial(_conv_module_kernel, nt=nt),
        grid=(bsz, nt),
        in_specs=[main(0), main(1), prev(0), prev(1), nxt(0), nxt(1),
                  const((4 * SUBLANES, d)), const((1, d)), const((1, d)), const((1, d))],
        out_specs=pl.BlockSpec((1, tt, d), lambda b, i: (b, i, 0)),
        out_shape=jax.ShapeDtypeStruct((bsz, t, d), F32),
        scratch_shapes=[pltpu.VMEM((tt + 2 * CONV_HALO, d), F32),
                        pltpu.VMEM((SUBLANES - 1, tt + 2 * CONV_HALO - SUBLANES, d), F32)],
        compiler_params=_cparams("parallel", "parallel"),
        name="conv_module",
    )(u3, u3, u3, u3, u3, u3, dw_w, dw_b, ln_w, ln_b)


def _attention_kernel(q_ref, k0_ref, k1_ref, k2_ref, v0_ref, v1_ref, v2_ref,
                      bias_ref, qw_ref, kw_ref, sink_ref, o_ref, *, nblk):
    i = pl.program_id(1)
    blk = ATT_BLOCK
    nqb = D_MODEL // LANES
    col = lax.broadcasted_iota(jnp.int32, (blk, 3 * blk), 1)
    oob = ((i == 0) & (col < blk)) | ((i == nblk - 1) & (col >= 2 * blk))
    lane_lo = lax.broadcasted_iota(jnp.int32, (blk, LANES), 1) < HEAD_DIM
    lane_lo3 = lax.broadcasted_iota(jnp.int32, (3 * blk, LANES), 1) < HEAD_DIM
    scale = HEAD_DIM ** -0.5

    def head_rms(x, w, lo):
        return x * lax.rsqrt(_head_sum(x * x, lo) * (1.0 / HEAD_DIM) + 1e-6) * w

    k_all = jnp.concatenate([k0_ref[0], k1_ref[0], k2_ref[0]], axis=0)
    v_all = jnp.concatenate([v0_ref[0], v1_ref[0], v2_ref[0]], axis=0)
    kn = [head_rms(k_all[:, kb * LANES:(kb + 1) * LANES], kw_ref[...], lane_lo3) for kb in range(2)]
    qn = [head_rms(q_ref[0, :, j * LANES:(j + 1) * LANES], qw_ref[...], lane_lo) for j in range(nqb)]
    half = nqb // 2
    logits = []
    for kb in range(2):
        stack = []
        for j in range(kb * half, (kb + 1) * half):
            q_lo = jnp.where(lane_lo, qn[j], 0.0)
            stack += [q_lo, qn[j] - q_lo]
        logits.append(_mm_nt(jnp.concatenate(stack, axis=0), kn[kb]))
    probs, rden = [], []
    for pos in range(N_HEADS):
        kb, idx = divmod(pos, 2 * half)
        lg = logits[kb][idx * blk:(idx + 1) * blk] * scale + bias_ref[pos]
        lg = jnp.where(oob, -jnp.inf, lg)
        sink = sink_ref[pos]
        m = jnp.maximum(jnp.max(lg, axis=-1, keepdims=True), sink)
        p = jnp.exp(lg - m)
        rden.append(1.0 / (jnp.sum(p, axis=-1, keepdims=True) + jnp.exp(sink - m)))
        probs.append(p.astype(BF16))
    for j in range(nqb):
        vb = v_all[:, (j // half) * LANES:(j // half + 1) * LANES]
        v_lo = jnp.where(lane_lo3, vb, 0.0)
        pv = _mm(jnp.concatenate([probs[2 * j], probs[2 * j + 1]], axis=1), jnp.concatenate([v_lo, vb - v_lo], axis=0))
        o_ref[0, :, j * LANES:(j + 1) * LANES] = pv * jnp.where(lane_lo, rden[2 * j], rden[2 * j + 1])


def _attention(u3, bias, q_norm_w, k_norm_w, sink):
    bsz, t, _ = u3.shape
    nblk = t // ATT_BLOCK
    d = D_MODEL
    kvw = ATT_KV_HEADS * HEAD_DIM
    kblk = (2 * D_MODEL + d) // kvw
    vblk = kblk + 1

    def kv_spec(blk, off):
        return pl.BlockSpec((1, ATT_BLOCK, kvw), lambda b, i: (b, jnp.clip(i + off, 0, nblk - 1), blk))

    def const(shape):
        return pl.BlockSpec(shape, lambda b, i: (0,) * len(shape))

    return pl.pallas_call(
        functools.partial(_attention_kernel, nblk=nblk),
        grid=(bsz, nblk),
        in_specs=[pl.BlockSpec((1, ATT_BLOCK, d), lambda b, i: (b, i, 2)),
                  kv_spec(kblk, -1), kv_spec(kblk, 0), kv_spec(kblk, 1),
                  kv_spec(vblk, -1), kv_spec(vblk, 0), kv_spec(vblk, 1),
                  const((N_HEADS, ATT_BLOCK, 3 * ATT_BLOCK)), const((1, LANES)), const((1, LANES)),
                  pl.BlockSpec(memory_space=pltpu.SMEM)],
        out_specs=pl.BlockSpec((1, ATT_BLOCK, d), lambda b, i: (b, i, 0)),
        out_shape=jax.ShapeDtypeStruct((bsz, t, d), F32),
        compiler_params=_cparams("parallel", "parallel"),
        name="attention",
    )(u3, u3, u3, u3, u3, u3, u3, bias, q_norm_w, k_norm_w, sink)


def _t5_bucket(rel):
    nb = REL_BUCKETS // 2
    max_exact = nb // 2
    n = jnp.abs(rel)
    nf = jnp.maximum(n, 1).astype(jnp.float32)
    large = max_exact + (jnp.log(nf / max_exact) / math.log(REL_MAX_DIST / max_exact)
                         * (nb - max_exact)).astype(jnp.int32)
    large = jnp.minimum(large, nb - 1)
    return (rel > 0).astype(jnp.int32) * nb + jnp.where(n < max_exact, n, large)


def _attention_bias(rel_bias):
    rel = jnp.arange(3 * ATT_BLOCK)[None, :] - ATT_BLOCK - jnp.arange(ATT_BLOCK)[:, None]
    bias = rel_bias.astype(F32)[_t5_bucket(rel)].transpose(2, 0, 1)
    bias = jnp.where((jnp.abs(rel) <= ATT_BLOCK)[None], bias, -jnp.inf)
    return bias[jnp.asarray(_ATT_HEAD_ORDER)]


_ATT_HEAD_ORDER = tuple(8 * (j // 4) + 4 * s + (j % 4) for j in range(8) for s in range(2))


def _permute_heads(w, axis):
    shape = w.shape
    split = shape[:axis] + (N_HEADS, HEAD_DIM) + shape[axis + 1:]
    return jnp.take(w.reshape(split), jnp.asarray(_ATT_HEAD_ORDER), axis=axis).reshape(shape)


def _pad_cols(w, width):
    return jnp.pad(w, ((0, 0), (0, width - w.shape[1])))


def _even_in_weight(w_in):
    z = w_in[:, :D_MODEL]
    xbc = w_in[:, D_MODEL:D_MODEL + SSD_XBC]
    dt = w_in[:, D_MODEL + SSD_XBC:D_MODEL + SSD_XBC + 2 * N_HEADS]
    p = w_in[:, D_MODEL + SSD_XBC + 2 * N_HEADS:]
    w = jnp.concatenate([p[:, :EV_RKV_W], xbc, p[:, EV_RKV_W:], _pad_cols(dt, 2 * LANES), z], axis=1)
    assert w.shape[1] == EV_W
    return w.astype(BF16)


def _row(v):
    return v.reshape(1, -1).astype(F32)


def _even_params(e, ev_w_in, ev_w_out, ssd_conv_w, ssd_conv_b, ssd_dt_bias, ssd_a_log, ssd_d, ssd_norm_w,
                 rwkv_mu, rwkv_w0, rwkv_w2, rwkv_a0, rwkv_a2, rwkv_g2, rwkv_k_k, rwkv_k_a, rwkv_r_k,
                 rwkv_ln_w, rwkv_ln_b):
    zeros = lambda n: jnp.zeros((n, D_MODEL), F32)
    lr_pad = lambda w, lo: jnp.concatenate([zeros(lo), w.astype(F32), zeros(EV_LR_W - lo - w.shape[0])], 0).astype(BF16)
    return dict(
        w_in=_even_in_weight(ev_w_in[e]),
        w_out=ev_w_out[e].astype(BF16),
        conv_w=jnp.pad(ssd_conv_w[e].astype(F32), ((0, SUBLANES - SSD_CONV), (0, 0))),
        conv_b=_row(ssd_conv_b[e]),
        dt_bias=_pad_cols(_row(ssd_dt_bias[e]), LANES),
        a_log=_pad_cols(_row(ssd_a_log[e]), LANES),
        d_skip=_row(jnp.repeat(ssd_d[e], HEAD_DIM)),
        ssd_norm_w=_row(ssd_norm_w[e]),
        mu_rkv=_row(rwkv_mu[e, :EV_RKV_W]),
        mu_lr=_row(rwkv_mu[e, EV_RKV_W:]),
        a0=_row(rwkv_a0[e]),
        w0=rwkv_w0[e].reshape(2, 1, D_MODEL).astype(F32),
        w2=jnp.stack([lr_pad(rwkv_w2[e, d], 0) for d in range(2)]),
        a2=lr_pad(rwkv_a2[e], 64),
        g2=lr_pad(rwkv_g2[e], 128),
        k_k=_row(rwkv_k_k[e]), k_a=_row(rwkv_k_a[e]), r_k=_row(rwkv_r_k[e]),
        ln_w=_row(rwkv_ln_w[e]), ln_b=_row(rwkv_ln_b[e]),
    )


def _odd_params(w_in, w_out, dw_w, dw_b, ln_w, ln_b, q_norm_w, k_norm_w, sink):
    q_lo, q_hi = 2 * D_MODEL, 3 * D_MODEL
    w_in = jnp.concatenate([w_in[:, :q_lo], _permute_heads(w_in[:, q_lo:q_hi], 1), w_in[:, q_hi:]], axis=1)
    w_out = jnp.concatenate([w_out[:D_MODEL], _permute_heads(w_out[D_MODEL:], 0)], axis=0)
    return dict(
        w_in=w_in.astype(BF16), w_out=w_out.astype(BF16),
        dw_w=jnp.pad(dw_w.astype(F32), ((0, 4 * SUBLANES - CONV_WIDTH), (0, 0))),
        dw_b=_row(dw_b), ln_w=_row(ln_w), ln_b=_row(ln_b),
        q_w=_row(jnp.tile(q_norm_w, 2)), k_w=_row(jnp.tile(k_norm_w, 2)),
        sink=sink.astype(F32)[jnp.asarray(_ATT_HEAD_ORDER)],
    )


def _even_mixers(x3, norm_w, p):
    bsz, t, d = x3.shape
    x2 = x3.reshape(bsz * t, d)
    u3 = _norm_matmul(x2, norm_w, p["w_in"], min(bsz * t, NORM_MM_ROWS), 512).reshape(bsz, t, EV_W)
    ya_f, xa = _ssd_fwd(u3, p["conv_w"], p["conv_b"], p["dt_bias"], p["a_log"])
    ya = _ssd_bwd(u3, xa, ya_f, p["dt_bias"], p["a_log"], p["d_skip"], p["ssd_norm_w"])
    a, lw0, lw1, g = _rwkv_prep(u3, p["mu_lr"], p["a0"], p["a2"], p["w0"], p["w2"], p["g2"], min(t, 256))
    yb_f, rkvs = _rwkv_scan_fwd(u3, p["mu_rkv"], a, lw0, p["k_k"], p["k_a"])
    yb = _rwkv_scan_bwd(rkvs, a, lw1, p["k_k"], p["k_a"], yb_f, g, p["r_k"], p["ln_w"], p["ln_b"])
    return ya, yb


def _odd_mixers(x3, norm_w, p, bias):
    bsz, t, d = x3.shape
    x2 = x3.reshape(bsz * t, d)
    u3 = _norm_matmul(x2, norm_w, p["w_in"], min(bsz * t, NORM_MM_ROWS), 512).reshape(bsz, t, -1)
    yc = _conv_module(u3, p["dw_w"], p["dw_b"], p["ln_w"], p["ln_b"], min(t, 256))
    yd = _attention(u3, bias, p["q_w"], p["k_w"], p["sink"])
    return yc, yd


def _trunk(x3, layers, bias):
    bsz, t, d = x3.shape
    tm = min(bsz * t, 512)
    for kind, norm_mix, p, norm_ffn, ffn_wi, ffn_wo in layers:
        ya, yb = _even_mixers(x3, norm_mix, p) if kind == "even" else _odd_mixers(x3, norm_mix, p, bias)
        x3 = _proj_ffn(x3.reshape(bsz * t, d), ya.reshape(bsz * t, d), yb.reshape(bsz * t, d), p["w_out"],
                       norm_ffn, ffn_wi, ffn_wo, tm).reshape(bsz, t, d)
    return x3


def kernel(x_prompt, x_sample, rel_bias, norm_mix_w, norm_ffn_w, ffn_w_in, ffn_w_out, ev_w_in, ev_w_out, ssd_conv_w, ssd_conv_b, ssd_dt_bias, ssd_a_log, ssd_d, ssd_norm_w, rwkv_mu, rwkv_w0, rwkv_w2, rwkv_a0, rwkv_a2, rwkv_g2, rwkv_k_k, rwkv_k_a, rwkv_r_k, rwkv_ln_w, rwkv_ln_b, od_w_in, od_w_out, conv_dw_w, conv_dw_b, conv_ln_w, conv_ln_b, att_q_norm_w, att_k_norm_w, att_sink):
    depth = norm_mix_w.shape[0]
    bias = _attention_bias(rel_bias)
    layers = []
    for layer in range(depth):
        i = layer // 2
        if layer % 2 == 0:
            kind = "even"
            p = _even_params(i, ev_w_in, ev_w_out, ssd_conv_w, ssd_conv_b, ssd_dt_bias, ssd_a_log, ssd_d,
                             ssd_norm_w, rwkv_mu, rwkv_w0, rwkv_w2, rwkv_a0, rwkv_a2, rwkv_g2, rwkv_k_k,
                             rwkv_k_a, rwkv_r_k, rwkv_ln_w, rwkv_ln_b)
        else:
            kind = "odd"
            p = _odd_params(od_w_in[i], od_w_out[i], conv_dw_w[i], conv_dw_b[i], conv_ln_w[i], conv_ln_b[i],
                            att_q_norm_w[i], att_k_norm_w[i], att_sink[i])
        layers.append((kind, _row(norm_mix_w[layer]), p, _row(norm_ffn_w[layer]),
                       ffn_w_in[layer].astype(BF16), ffn_w_out[layer].astype(BF16)))
    return (_trunk(x_prompt, layers, bias), _trunk(x_sample, layers, bias))
```

```python
import functools
import math

import jax
import jax.numpy as jnp
import numpy as np
from jax import lax
from jax.experimental import pallas as pl
from jax.experimental.pallas import tpu as pltpu

F32 = jnp.float32
BF16 = jnp.bfloat16
HIGHEST = lax.Precision.HIGHEST

LANES = 128
SUBLANES = 8
VMEM_LIMIT_BYTES = 56 * 1024 * 1024

D_MODEL = 1024
HEAD_DIM = 64
D_FF = 2816
FF_CHUNK = 256
NORM_MM_ROWS = 2048
N_HEADS = 16
SSD_GROUPS = 2
SSD_STATE = 128
SSD_CONV = 5
SSD_XBC = D_MODEL + 2 * SSD_GROUPS * SSD_STATE
CHUNK = 128
RWKV_LN_EPS = 64e-5
CONV_WIDTH = 31
CONV_HALO = 16
ATT_BLOCK = 128
ATT_KV_HEADS = 4
ATT_GQ = 4
REL_BUCKETS = 32
REL_MAX_DIST = 128

EV_W = 6144
EV_RKV_W, EV_XBC_W, EV_LR_W, EV_DT_W, EV_Z_W = 3072, SSD_XBC, 256, LANES, D_MODEL
EV_XBC_BLK = 3072 // EV_XBC_W
EV_LR_BLK = 4608 // EV_LR_W
EV_DT_BLK = 4864 // EV_DT_W
EV_Z_BLK = 5120 // EV_Z_W


def _cparams(*sem):
    return pltpu.CompilerParams(dimension_semantics=sem, vmem_limit_bytes=VMEM_LIMIT_BYTES)


def _mm(a, b):
    return jnp.dot(a.astype(BF16), b.astype(BF16), preferred_element_type=F32)


def _mm_nt(a, b):
    return lax.dot_general(a.astype(BF16), b.astype(BF16), (((1,), (1,)), ((), ())),
                           preferred_element_type=F32)


def _mm_tn(a, b):
    return lax.dot_general(a.astype(BF16), b.astype(BF16), (((0,), (0,)), ((), ())),
                           preferred_element_type=F32)


def _rms(x, w, eps=1e-6):
    return x * lax.rsqrt(jnp.mean(x * x, axis=-1, keepdims=True) + eps) * w


def _norm_matmul_kernel(x_ref, nw_ref, w_ref, o_ref, h_ref):
    @pl.when(pl.program_id(1) == 0)
    def _():
        h_ref[...] = _rms(x_ref[...], nw_ref[...]).astype(BF16)

    o_ref[...] = jnp.dot(h_ref[...], w_ref[...], preferred_element_type=F32)


def _norm_matmul(x2, nw, w, tm, tn):
    n, d = x2.shape
    nout = w.shape[1]
    return pl.pallas_call(
        _norm_matmul_kernel,
        grid=(n // tm, nout // tn),
        in_specs=[pl.BlockSpec((tm, d), lambda i, j: (i, 0)),
                  pl.BlockSpec((1, d), lambda i, j: (0, 0)),
                  pl.BlockSpec((d, tn), lambda i, j: (0, j))],
        out_specs=pl.BlockSpec((tm, tn), lambda i, j: (i, j)),
        out_shape=jax.ShapeDtypeStruct((n, nout), F32),
        scratch_shapes=[pltpu.VMEM((tm, d), BF16)],
        compiler_params=_cparams("parallel", "arbitrary"),
        name="norm_matmul",
    )(x2, nw, w)


def _proj_ffn_kernel(x_ref, a_ref, b_ref, wm_ref, nw_ref, wi_ref, wo_ref, o_ref):
    d = x_ref.shape[1]
    x = x_ref[...] + _mm(a_ref[...], wm_ref[0:d, :]) + _mm(b_ref[...], wm_ref[d:2 * d, :])
    h = _rms(x, nw_ref[...]).astype(BF16)
    o_ref[...] = x
    for c in range(D_FF // FF_CHUNK):
        lo = c * FF_CHUNK
        g = jnp.dot(h, wi_ref[:, lo:lo + FF_CHUNK], preferred_element_type=F32)
        u = jnp.dot(h, wi_ref[:, D_FF + lo:D_FF + lo + FF_CHUNK], preferred_element_type=F32)
        act = (g * jax.nn.sigmoid(g) * u).astype(BF16)
        o_ref[...] += jnp.dot(act, wo_ref[lo:lo + FF_CHUNK, :], preferred_element_type=F32)


def _proj_ffn(x2, a2, b2, w_mix, nw, wi, wo, tm):
    n, d = x2.shape
    row = pl.BlockSpec((tm, d), lambda i: (i, 0))
    resident = lambda shape: pl.BlockSpec(shape, lambda i: (0, 0), pipeline_mode=pl.Buffered(1))
    return pl.pallas_call(
        _proj_ffn_kernel,
        grid=(n // tm,),
        in_specs=[row, row, row, resident((2 * d, d)), pl.BlockSpec((1, d), lambda i: (0, 0)),
                  resident((d, 2 * D_FF)), resident((D_FF, d))],
        out_specs=row,
        out_shape=jax.ShapeDtypeStruct((n, d), F32),
        compiler_params=_cparams("parallel"),
        name="proj_ffn",
    )(x2, a2, b2, w_mix, nw, wi, wo)


def _shift_mix(p, prev_row, next_row, mu):
    tt = p.shape[0]
    row8 = lax.broadcasted_iota(jnp.int32, (SUBLANES, p.shape[1]), 0)
    prev = pltpu.roll(p, 1, 0)
    prev = jnp.concatenate([jnp.where(row8 == 0, prev_row, prev[:SUBLANES]), prev[SUBLANES:]], axis=0)
    nxt = pltpu.roll(p, tt - 1, 0)
    nxt = jnp.concatenate([nxt[:tt - SUBLANES], jnp.where(row8 == SUBLANES - 1, next_row, nxt[tt - SUBLANES:])],
                          axis=0)
    return p + mu * (0.5 * (prev + nxt) - p)


def _rwkv_prep_kernel(lr_ref, lr_p_ref, lr_n_ref, mu_lr_ref, a0_ref, a2_ref, w0_ref, w2_ref, g2_ref,
                      a_ref, lw0_ref, lw1_ref, g_ref, *, nt):
    t = pl.program_id(1)
    keep_p = (t > 0).astype(F32)
    keep_n = (t < nt - 1).astype(F32)
    lr = _shift_mix(lr_ref[0], lr_p_ref[0, 7:8, :] * keep_p, lr_n_ref[0, 0:1, :] * keep_n, mu_lr_ref[...])
    a_ref[0] = jax.nn.sigmoid(a0_ref[...] + _mm(lr, a2_ref[...]))
    g_ref[0] = _mm(jax.nn.sigmoid(lr), g2_ref[...])
    th = jnp.tanh(lr)
    for d, out in enumerate((lw0_ref, lw1_ref)):
        wlog = -jax.nn.softplus(-(w0_ref[d] + _mm(th, w2_ref[d]))) - 0.5
        out[0] = -jnp.exp(wlog)


def _rwkv_prep(u3, mu_lr, a0, a2p, w0, w2p, g2p, tt):
    bsz, t, _ = u3.shape
    nt = t // tt
    r8 = tt // SUBLANES
    last8 = t // SUBLANES - 1
    d = D_MODEL

    def const(shape):
        return pl.BlockSpec(shape, lambda b, i: (0,) * len(shape))

    out = pl.BlockSpec((1, tt, d), lambda b, i: (b, i, 0))
    sds = jax.ShapeDtypeStruct((bsz, t, d), F32)
    return pl.pallas_call(
        functools.partial(_rwkv_prep_kernel, nt=nt),
        grid=(bsz, nt),
        in_specs=[pl.BlockSpec((1, tt, EV_LR_W), lambda b, i: (b, i, EV_LR_BLK)),
                  pl.BlockSpec((1, SUBLANES, EV_LR_W), lambda b, i: (b, jnp.maximum(i * r8 - 1, 0), EV_LR_BLK)),
                  pl.BlockSpec((1, SUBLANES, EV_LR_W), lambda b, i: (b, jnp.minimum((i + 1) * r8, last8), EV_LR_BLK)),
                  const((1, EV_LR_W)), const((1, d)), const((EV_LR_W, d)),
                  const((2, 1, d)), const((2, EV_LR_W, d)), const((EV_LR_W, d))],
        out_specs=[out, out, out, out],
        out_shape=[sds, sds, sds, sds],
        compiler_params=_cparams("parallel", "parallel"),
        name="rwkv_prep",
    )(u3, u3, u3, mu_lr, a0, a2p, w0, w2p, g2p)


def _head_sum(x, lane_lo):
    s0 = jnp.sum(jnp.where(lane_lo, x, 0.0), axis=-1, keepdims=True)
    s1 = jnp.sum(jnp.where(lane_lo, 0.0, x), axis=-1, keepdims=True)
    return jnp.where(lane_lo, s0, s1)


def _same_block(ti, si, log2_size):
    return jnp.right_shift(ti, log2_size) == jnp.right_shift(si, log2_size)


def _unit_lower_inverses(a_list, ti, si, eye):
    sh = 4
    n = a_list[0].shape[0]
    x = [jnp.where(_same_block(ti, si, sh), a, 0.0) for a in a_list]
    t_inv = [eye + xi for xi in x]
    x = [_mm(xi, xi) for xi in x]
    yield
    for step in range(sh - 1):
        last = step == sh - 2
        prod = [_mm(t if last else jnp.concatenate([t, xi], axis=0), xi) for t, xi in zip(t_inv, x)]
        t_inv = [t + p[:n] for t, p in zip(t_inv, prod)]
        x = [p[n:] for p in prod]
        yield
    while (1 << sh) < a_list[0].shape[0]:
        off = jnp.logical_and(jnp.logical_not(_same_block(ti, si, sh)), _same_block(ti, si, sh + 1))
        at = [_mm(jnp.where(off, a, 0.0), t) for a, t in zip(a_list, t_inv)]
        yield
        t_inv = [t + _mm(t, m) for t, m in zip(t_inv, at)]
        yield
        sh += 1
    return t_inv


def _cumsum_rows(tri_bf16, x):
    hi = x.astype(BF16)
    lo = (x - hi.astype(F32)).astype(BF16)
    return (jnp.dot(tri_bf16, hi, preferred_element_type=F32)
            + jnp.dot(tri_bf16, lo, preferred_element_type=F32))


def _rwkv_scan_body(refs, rev, nc):
    final = rev
    pairs = D_MODEL // LANES
    if final:
        (r_ref, k_ref, v_ref, a_ref, lw_ref, kk_w_ref, ka_w_ref,
         yf_ref, g_ref, rk_w_ref, lnw_ref, lnb_ref, o_ref, s_ref) = refs
    else:
        (r_ref, k_ref, v_ref, rp_ref, kp_ref, vp_ref, rn_ref, kn_ref, vn_ref, mu_ref,
         a_ref, lw_ref, kk_w_ref, ka_w_ref, o_ref, rkvs_ref, s_ref) = refs
    L = CHUNK
    c_id = pl.program_id(1)

    if final:
        r_all, k_all, v_all = r_ref[0], k_ref[0], v_ref[0]
    else:
        cc = nc - 1 - c_id if rev else c_id
        keep_p = (cc > 0).astype(F32)
        keep_n = (cc < nc - 1).astype(F32)
        mixed = []
        for i, (m_ref, p_ref, n_ref) in enumerate(((r_ref, rp_ref, rn_ref), (k_ref, kp_ref, kn_ref),
                                                   (v_ref, vp_ref, vn_ref))):
            x = _shift_mix(m_ref[0], p_ref[0, SUBLANES - 1:SUBLANES, :] * keep_p, n_ref[0, 0:1, :] * keep_n,
                           mu_ref[:, i * D_MODEL:(i + 1) * D_MODEL])
            rkvs_ref[0, :, i * D_MODEL:(i + 1) * D_MODEL] = x
            mixed.append(x)
        r_all, k_all, v_all = mixed
    yield

    lane_lo = lax.broadcasted_iota(jnp.int32, (L, LANES), 1) < HEAD_DIM
    ti = lax.broadcasted_iota(jnp.int32, (L, L), 0)
    si = lax.broadcasted_iota(jnp.int32, (L, L), 1)
    tri = (si >= ti) if rev else (si <= ti)
    strict = (si > ti) if rev else (si < ti)
    eye = (si == ti).astype(F32)
    heads = [(q, h) for q in range(pairs) for h in range(2)]
    cols = [slice(q * LANES, (q + 1) * LANES) for q in range(pairs)]
    own = lambda h, x: jnp.where(lane_lo, x, 0.0) if h == 0 else jnp.where(lane_lo, 0.0, x)

    lw_all = lw_ref[0]
    cum_all = _cumsum_rows(jnp.where(tri, 1.0, 0.0).astype(BF16), lw_all)
    r = [r_all[:, c] for c in cols]
    v = [v_all[:, c] for c in cols]
    kmod, a_t, b_t, k_t, r_t, b_h, k_h, tot = [], [], [], [], [], [], [], []
    for q, c in enumerate(cols):
        k, a, cum, lw = k_all[:, c], a_ref[0, :, c], cum_all[:, c], lw_all[:, c]
        kk = k * kk_w_ref[:, c]
        kk = kk / jnp.maximum(jnp.sqrt(_head_sum(kk * kk, lane_lo)), 1e-12)
        km = k * (1.0 + (a - 1.0) * ka_w_ref[:, c])
        ib = kk * a
        tq = cum[0:1, :] if rev else cum[L - 1:L, :]
        p_inv = jnp.exp(-cum)
        p_end = jnp.exp(tq - cum)
        kmod.append(km)
        tot.append(tq)
        a_t.append(-kk * jnp.exp(cum - lw))
        b_t.append(ib * p_inv)
        k_t.append(km * p_inv)
        r_t.append(r[q] * jnp.exp(cum))
        b_h.append(ib * p_end)
        k_h.append(km * p_end)
        if q % 2 == 1:
            yield

    a_m = [own(h, a_t[q]) for q, h in heads]
    gram = [_mm_nt(jnp.concatenate([a_m[i], own(h, r_t[q])], axis=0),
                   jnp.concatenate([b_t[q], k_t[q]], axis=0)) for i, (q, h) in enumerate(heads)]
    yield
    a_ab = [jnp.where(strict, g[:L, :L], 0.0) for g in gram]
    av = [_mm(jnp.where(strict, g[:L, L:], 0.0), own(h, v[q])) for g, (q, h) in zip(gram, heads)]
    m_rb = [jnp.where(tri, g[L:, :L], 0.0) for g in gram]
    m_rk = [jnp.where(tri, g[L:, L:], 0.0) for g in gram]
    yield
    t_inv = yield from _unit_lower_inverses(a_ab, ti, si, eye)

    w12 = [_mm(jnp.concatenate([t_inv[2 * q], t_inv[2 * q + 1]], axis=1),
               jnp.concatenate([jnp.concatenate([a_m[2 * q], av[2 * q]], axis=1),
                                jnp.concatenate([a_m[2 * q + 1], av[2 * q + 1]], axis=1)], axis=0))
           for q in range(pairs)]
    yield
    s0 = [s_ref[q] for q in range(pairs)]
    u = [_mm_nt(w12[q][:, :LANES], s0[q]) + w12[q][:, LANES:] for q in range(pairs)]
    yield
    y = []
    for q in range(pairs):
        u_lo, v_lo = own(0, u[q]), own(0, v[q])
        uv = jnp.concatenate([u_lo, u[q] - u_lo, v_lo, v[q] - v_lo], axis=0)
        m_cat = jnp.concatenate([m_rb[2 * q], m_rb[2 * q + 1], m_rk[2 * q], m_rk[2 * q + 1]], axis=1)
        y.append(_mm_nt(r_t[q], s0[q]) + _mm(m_cat, uv))
    yield
    for q in range(pairs):
        s_new = s0[q] * jnp.exp(tot[q]) + _mm_tn(jnp.concatenate([u[q], v[q]], axis=0),
                                                 jnp.concatenate([b_h[q], k_h[q]], axis=0))
        s_ref[q] = jnp.where(_same_block(ti, si, 6), s_new, 0.0)
    yield

    for q, c in enumerate(cols):
        if not final:
            o_ref[0, :, c] = y[q]
        else:
            yq = y[q] + yf_ref[0, :, c]
            inv_n = 1.0 / HEAD_DIM
            yc = yq - _head_sum(yq, lane_lo) * inv_n
            yn = yc * lax.rsqrt(_head_sum(yc * yc, lane_lo) * inv_n + RWKV_LN_EPS)
            bonus = _head_sum(r[q] * kmod[q] * rk_w_ref[:, c], lane_lo)
            o_ref[0, :, c] = (yn * lnw_ref[:, c] + lnb_ref[:, c] + bonus * v[q]) * g_ref[0, :, c]


class _ScanPart:
    def __init__(self, in_specs, args, out_specs, out_shape, scratch):
        self.in_specs, self.args, self.out_specs, self.out_shape, self.scratch = (
            list(in_specs), list(args), list(out_specs), list(out_shape), list(scratch))


def _chunk_specs(t, rev):
    nc = t // CHUNK
    r8 = CHUNK // SUBLANES
    last8 = t // SUBLANES - 1
    cmap = (lambda c: nc - 1 - c) if rev else (lambda c: c)
    blk = lambda w, j: pl.BlockSpec((1, CHUNK, w), lambda b, c: (b, cmap(c), j))
    prev = lambda w, j: pl.BlockSpec((1, SUBLANES, w), lambda b, c: (b, jnp.maximum(cmap(c) * r8 - 1, 0), j))
    nxt = lambda w, j: pl.BlockSpec((1, SUBLANES, w), lambda b, c: (b, jnp.minimum((cmap(c) + 1) * r8, last8), j))
    const = lambda shape: pl.BlockSpec(shape, lambda b, c: (0,) * len(shape))
    return blk, prev, nxt, const


def _rwkv_fwd_part(u3, mu_rkv, a, lw, kk_w, ka_w):
    bsz, t, _ = u3.shape
    d = D_MODEL
    blk, prev, nxt, const = _chunk_specs(t, False)
    return _ScanPart(
        [blk(d, 0), blk(d, 1), blk(d, 2), prev(d, 0), prev(d, 1), prev(d, 2), nxt(d, 0), nxt(d, 1), nxt(d, 2),
         const((1, EV_RKV_W)), blk(d, 0), blk(d, 0), const((1, d)), const((1, d))],
        [u3] * 9 + [mu_rkv, a, lw, kk_w, ka_w],
        [blk(d, 0), blk(EV_RKV_W, 0)],
        [jax.ShapeDtypeStruct((bsz, t, d), F32), jax.ShapeDtypeStruct((bsz, t, EV_RKV_W), F32)],
        [pltpu.VMEM((d // LANES, LANES, LANES), F32)])


def _rwkv_bwd_part(rkvs, a, lw, kk_w, ka_w, yf, g, rk_w, ln_w, ln_b):
    bsz, t, _ = rkvs.shape
    d = D_MODEL
    blk, _, _, const = _chunk_specs(t, True)
    par = const((1, d))
    return _ScanPart(
        [blk(d, 0), blk(d, 1), blk(d, 2), blk(d, 0), blk(d, 0), par, par, blk(d, 0), blk(d, 0), par, par, par],
        [rkvs, rkvs, rkvs, a, lw, kk_w, ka_w, yf, g, rk_w, ln_w, ln_b],
        [blk(d, 0)], [jax.ShapeDtypeStruct((bsz, t, d), F32)],
        [pltpu.VMEM((d // LANES, LANES, LANES), F32)])


_DONE = object()


def _even_scan_kernel(*refs, rev, nc, counts):
    groups, pos = [], 0
    for n in counts:
        groups.append(list(refs[pos:pos + n]))
        pos += n
    rwkv_in, ssd_in, rwkv_out, ssd_out, rwkv_scr, ssd_scr = groups

    @pl.when(pl.program_id(1) == 0)
    def _():
        rwkv_scr[0][...] = jnp.zeros_like(rwkv_scr[0])
        ssd_scr[0][...] = jnp.zeros_like(ssd_scr[0])

    bodies = [_rwkv_scan_body(rwkv_in + rwkv_out + rwkv_scr, rev, nc),
              _ssd_body(ssd_in + ssd_out + ssd_scr, rev, nc)]
    while bodies:
        for body in list(bodies):
            if next(body, _DONE) is _DONE:
                bodies.remove(body)


def _even_scan(rwkv, ssd, bsz, t, rev):
    counts = (len(rwkv.in_specs), len(ssd.in_specs), len(rwkv.out_specs), len(ssd.out_specs),
              len(rwkv.scratch), len(ssd.scratch))
    return pl.pallas_call(
        functools.partial(_even_scan_kernel, rev=rev, nc=t // CHUNK, counts=counts),
        grid=(bsz, t // CHUNK),
        in_specs=rwkv.in_specs + ssd.in_specs,
        out_specs=rwkv.out_specs + ssd.out_specs,
        out_shape=rwkv.out_shape + ssd.out_shape,
        scratch_shapes=rwkv.scratch + ssd.scratch,
        compiler_params=_cparams("parallel", "arbitrary"),
        name="even_scan_bwd" if rev else "even_scan_fwd",
    )(*(rwkv.args + ssd.args))


def _split_dot(x, w_bf16, terms, lhs=False):
    out, rem = None, x
    for _ in range(terms):
        piece = rem.astype(BF16)
        rem = rem - piece.astype(F32)
        part = (jnp.dot(w_bf16, piece, preferred_element_type=F32) if lhs
                else jnp.dot(piece, w_bf16, preferred_element_type=F32))
        out = part if out is None else out + part
    return out


def _ssd_body(refs, rev, nc):
    if rev:
        (xa_ref, dt_ref, dtb_ref, alog_ref, ex_ref, z_ref, yf_ref, dsk_ref, nw_ref, o_ref, h_ref) = refs
    else:
        (xbc_ref, xp_ref, xn_ref, dt_ref, cw_ref, cb_ref, dtb_ref, alog_ref, ex_ref,
         o_ref, xa_out_ref, h_ref, pad_ref) = refs
    L = CHUNK
    c = pl.program_id(1)
    cc = nc - 1 - c if rev else c
    gw = D_MODEL // SSD_GROUPS

    if rev:
        xa = xa_ref[0]
    else:
        pad_ref[0:SUBLANES, :] = xp_ref[0] * (cc > 0).astype(F32)
        pad_ref[SUBLANES:SUBLANES + L, :] = xbc_ref[0]
        pad_ref[SUBLANES + L:2 * SUBLANES + L, :] = xn_ref[0] * (cc < nc - 1).astype(F32)
        pieces = []
        for lo_c in range(0, EV_XBC_W, gw):
            cs_ = slice(lo_c, lo_c + gw)
            acc = cb_ref[:, cs_] + cw_ref[0:1, cs_] * pad_ref[SUBLANES - 2:SUBLANES - 2 + L, cs_]
            for j in range(1, SSD_CONV):
                lo = SUBLANES - 2 + j
                acc = acc + cw_ref[j:j + 1, cs_] * pad_ref[lo:lo + L, cs_]
            piece = acc * jax.nn.sigmoid(acc)
            xa_out_ref[0, :, cs_] = piece
            pieces.append(piece)
            yield
        xa = jnp.concatenate(pieces, axis=1)
    xs = xa[:, :D_MODEL]
    n_bc = SSD_GROUPS * SSD_STATE

    dtv = jax.nn.softplus(dt_ref[0] + dtb_ref[...])
    da = dtv * -jnp.exp(alog_ref[...])
    ti = lax.broadcasted_iota(jnp.int32, (L, L), 0)
    si = lax.broadcasted_iota(jnp.int32, (L, L), 1)
    tri = (si >= ti) if rev else (si <= ti)
    cs = _split_dot(da, jnp.where(tri, 1.0, 0.0).astype(BF16), 3, lhs=True)
    cs_t = cs.T
    edge = 0 if rev else L - 1
    ex = ex_ref[...]
    yield
    dt_e = _split_dot(dtv, ex, 2)
    cs_e = _split_dot(cs, ex, 3)
    yield
    tot_e = cs_e[edge:edge + 1, :]
    xd = xs * dt_e
    xst = xd * jnp.exp(tot_e - cs_e)
    ecs = jnp.exp(cs_e)
    yield

    lane_lo = lax.broadcasted_iota(jnp.int32, (L, LANES), 1) < HEAD_DIM
    bm = [xa[:, D_MODEL + g * SSD_STATE:D_MODEL + (g + 1) * SSD_STATE] for g in range(SSD_GROUPS)]
    cm = [xa[:, D_MODEL + n_bc + g * SSD_STATE:D_MODEL + n_bc + (g + 1) * SSD_STATE] for g in range(SSD_GROUPS)]
    cb = [_mm_nt(cm[g], bm[g]) for g in range(SSD_GROUPS)]
    h0 = [h_ref[g] for g in range(SSD_GROUPS)]
    y_off = [_mm(cm[g], h0[g]) for g in range(SSD_GROUPS)]
    st = [_mm(bm[g].T, xst[:, g * gw:(g + 1) * gw]) for g in range(SSD_GROUPS)]
    for g in range(SSD_GROUPS):
        h_ref[g] = h0[g] * jnp.exp(tot_e[:, g * gw:(g + 1) * gw]) + st[g]
    yield

    off = N_HEADS if rev else 0
    y_diag = []
    for p in range(N_HEADS // 2):
        gmat = []
        for hh in (2 * p, 2 * p + 1):
            col = cs[:, off + hh:off + hh + 1]
            row = cs_t[off + hh:off + hh + 1, :]
            decay = jnp.exp(jnp.where(tri, col - row, -jnp.inf))
            gmat.append((cb[hh // (N_HEADS // SSD_GROUPS)] * decay).astype(BF16))
        xp = xd[:, p * LANES:(p + 1) * LANES]
        x_lo = jnp.where(lane_lo, xp, 0.0)
        rhs = jnp.concatenate([x_lo, xp - x_lo], axis=0)
        y_diag.append(_mm(jnp.concatenate(gmat, axis=1), rhs))
        yield
    y = jnp.concatenate(y_diag, axis=1) + jnp.concatenate(y_off, axis=1) * ecs

    if not rev:
        o_ref[0] = y
    else:
        z = z_ref[0]
        y = (y + yf_ref[0] + dsk_ref[...] * xs) * (z * jax.nn.sigmoid(z))
        parts = []
        for g in range(SSD_GROUPS):
            yg = y[:, g * gw:(g + 1) * gw]
            parts.append(yg * lax.rsqrt(jnp.mean(yg * yg, axis=-1, keepdims=True) + 1e-6))
        o_ref[0] = jnp.concatenate(parts, axis=1) * nw_ref[...]


def _head_expand_matrix(rev):
    e = np.zeros((LANES, D_MODEL), np.float32)
    for hh in range(N_HEADS):
        e[(N_HEADS if rev else 0) + hh, hh * HEAD_DIM:(hh + 1) * HEAD_DIM] = 1.0
    return jnp.asarray(e, dtype=BF16)


def _ssd_fwd_part(u3, conv_w, conv_b, dt_bias, a_log):
    bsz, t, _ = u3.shape
    blk, prev, nxt, const = _chunk_specs(t, False)
    return _ScanPart(
        [blk(EV_XBC_W, EV_XBC_BLK), prev(EV_XBC_W, EV_XBC_BLK), nxt(EV_XBC_W, EV_XBC_BLK), blk(EV_DT_W, EV_DT_BLK),
         const((SUBLANES, EV_XBC_W)), const((1, EV_XBC_W)), const((1, LANES)), const((1, LANES)),
         const((LANES, D_MODEL))],
        [u3, u3, u3, u3, conv_w, conv_b, dt_bias, a_log, _head_expand_matrix(False)],
        [blk(D_MODEL, 0), blk(EV_XBC_W, 0)],
        [jax.ShapeDtypeStruct((bsz, t, D_MODEL), F32), jax.ShapeDtypeStruct((bsz, t, EV_XBC_W), F32)],
        [pltpu.VMEM((SSD_GROUPS, SSD_STATE, D_MODEL // SSD_GROUPS), F32),
         pltpu.VMEM((CHUNK + 2 * SUBLANES, EV_XBC_W), F32)])


def _ssd_bwd_part(u3, xa, yf, dt_bias, a_log, d_skip, norm_w):
    bsz, t, _ = u3.shape
    blk, _, _, const = _chunk_specs(t, True)
    return _ScanPart(
        [blk(EV_XBC_W, 0), blk(EV_DT_W, EV_DT_BLK), const((1, LANES)), const((1, LANES)), const((LANES, D_MODEL)),
         blk(EV_Z_W, EV_Z_BLK), blk(D_MODEL, 0), const((1, D_MODEL)), const((1, D_MODEL))],
        [xa, u3, dt_bias, a_log, _head_expand_matrix(True), u3, yf, d_skip, norm_w],
        [blk(D_MODEL, 0)], [jax.ShapeDtypeStruct((bsz, t, D_MODEL), F32)],
        [pltpu.VMEM((SSD_GROUPS, SSD_STATE, D_MODEL // SSD_GROUPS), F32)])


def _conv_module_kernel(val_ref, gate_ref, vp_ref, gp_ref, vn_ref, gn_ref,
                        dw_ref, db_ref, lnw_ref, lnb_ref, o_ref, pad_ref, sh_ref, *, nt):
    t = pl.program_id(1)
    tt = val_ref.shape[1]
    h = CONV_HALO
    pad_ref[0:h, :] = vp_ref[0] * jax.nn.sigmoid(gp_ref[0]) * (t > 0).astype(F32)
    pad_ref[h:h + tt, :] = val_ref[0] * jax.nn.sigmoid(gate_ref[0])
    pad_ref[h + tt:2 * h + tt, :] = vn_ref[0] * jax.nn.sigmoid(gn_ref[0]) * (t < nt - 1).astype(F32)
    rows = tt + 2 * h - SUBLANES
    for b in range(1, SUBLANES):
        sh_ref[b - 1] = pad_ref[b:b + rows, :]
    acc = db_ref[...]
    for j in range(CONV_WIDTH):
        a, b = divmod(h - CONV_WIDTH // 2 + j, SUBLANES)
        lo = a * SUBLANES
        tap = pad_ref[lo:lo + tt, :] if b == 0 else sh_ref[b - 1, lo:lo + tt, :]
        acc = acc + dw_ref[j:j + 1, :] * tap
    xc = acc - jnp.mean(acc, axis=-1, keepdims=True)
    xn = xc * lax.rsqrt(jnp.mean(xc * xc, axis=-1, keepdims=True) + 1e-5)
    yv = xn * lnw_ref[...] + lnb_ref[...]
    o_ref[0] = yv * jax.nn.sigmoid(yv)


def _conv_module(u3, dw_w, dw_b, ln_w, ln_b, tt):
    bsz, t, _ = u3.shape
    nt = t // tt
    rh = tt // CONV_HALO
    lasth = t // CONV_HALO - 1
    d = D_MODEL

    def const(shape):
        return pl.BlockSpec(shape, lambda b, i: (0,) * len(shape))

    main = lambda blk: pl.BlockSpec((1, tt, d), lambda b, i: (b, i, blk))
    prev = lambda blk: pl.BlockSpec((1, CONV_HALO, d), lambda b, i: (b, jnp.maximum(i * rh - 1, 0), blk))
    nxt = lambda blk: pl.BlockSpec((1, CONV_HALO, d), lambda b, i: (b, jnp.minimum((i + 1) * rh, lasth), blk))
    return pl.pallas_call(
        functools.partial(_conv_module_kernel, nt=nt),
        grid=(bsz, nt),
        in_specs=[main(0), main(1), prev(0), prev(1), nxt(0), nxt(1),
                  const((4 * SUBLANES, d)), const((1, d)), const((1, d)), const((1, d))],
        out_specs=pl.BlockSpec((1, tt, d), lambda b, i: (b, i, 0)),
        out_shape=jax.ShapeDtypeStruct((bsz, t, d), F32),
        scratch_shapes=[pltpu.VMEM((tt + 2 * CONV_HALO, d), F32),
                        pltpu.VMEM((SUBLANES - 1, tt + 2 * CONV_HALO - SUBLANES, d), F32)],
        compiler_params=_cparams("parallel", "parallel"),
        name="conv_module",
    )(u3, u3, u3, u3, u3, u3, dw_w, dw_b, ln_w, ln_b)


def _attention_kernel(q_ref, k0_ref, k1_ref, k2_ref, v0_ref, v1_ref, v2_ref,
                      bias_ref, qw_ref, kw_ref, sink_ref, o_ref, *, nblk):
    i = pl.program_id(1)
    blk = ATT_BLOCK
    nqb = D_MODEL // LANES
    col = lax.broadcasted_iota(jnp.int32, (blk, 3 * blk), 1)
    oob = ((i == 0) & (col < blk)) | ((i == nblk - 1) & (col >= 2 * blk))
    lane_lo = lax.broadcasted_iota(jnp.int32, (blk, LANES), 1) < HEAD_DIM
    lane_lo3 = lax.broadcasted_iota(jnp.int32, (3 * blk, LANES), 1) < HEAD_DIM
    scale = HEAD_DIM ** -0.5

    def head_rms(x, w, lo):
        return x * lax.rsqrt(_head_sum(x * x, lo) * (1.0 / HEAD_DIM) + 1e-6) * w

    k_all = jnp.concatenate([k0_ref[0], k1_ref[0], k2_ref[0]], axis=0)
    v_all = jnp.concatenate([v0_ref[0], v1_ref[0], v2_ref[0]], axis=0)
    kn = [head_rms(k_all[:, kb * LANES:(kb + 1) * LANES], kw_ref[...], lane_lo3) for kb in range(2)]
    qn = [head_rms(q_ref[0, :, j * LANES:(j + 1) * LANES], qw_ref[...], lane_lo) for j in range(nqb)]
    half = nqb // 2
    logits = []
    for kb in range(2):
        stack = []
        for j in range(kb * half, (kb + 1) * half):
            q_lo = jnp.where(lane_lo, qn[j], 0.0)
            stack += [q_lo, qn[j] - q_lo]
        logits.append(_mm_nt(jnp.concatenate(stack, axis=0), kn[kb]))
    probs, rden = [], []
    for pos in range(N_HEADS):
        kb, idx = divmod(pos, 2 * half)
        lg = logits[kb][idx * blk:(idx + 1) * blk] * scale + bias_ref[pos]
        lg = jnp.where(oob, -jnp.inf, lg)
        sink = sink_ref[pos]
        m = jnp.maximum(jnp.max(lg, axis=-1, keepdims=True), sink)
        p = jnp.exp(lg - m)
        rden.append(1.0 / (jnp.sum(p, axis=-1, keepdims=True) + jnp.exp(sink - m)))
        probs.append(p.astype(BF16))
    for j in range(nqb):
        vb = v_all[:, (j // half) * LANES:(j // half + 1) * LANES]
        v_lo = jnp.where(lane_lo3, vb, 0.0)
        pv = _mm(jnp.concatenate([probs[2 * j], probs[2 * j + 1]], axis=1), jnp.concatenate([v_lo, vb - v_lo], axis=0))
        o_ref[0, :, j * LANES:(j + 1) * LANES] = pv * jnp.where(lane_lo, rden[2 * j], rden[2 * j + 1])


def _attention(u3, bias, q_norm_w, k_norm_w, sink):
    bsz, t, _ = u3.shape
    nblk = t // ATT_BLOCK
    d = D_MODEL
    kvw = ATT_KV_HEADS * HEAD_DIM
    kblk = (2 * D_MODEL + d) // kvw
    vblk = kblk + 1

    def kv_spec(blk, off):
        return pl.BlockSpec((1, ATT_BLOCK, kvw), lambda b, i: (b, jnp.clip(i + off, 0, nblk - 1), blk))

    def const(shape):
        return pl.BlockSpec(shape, lambda b, i: (0,) * len(shape))

    return pl.pallas_call(
        functools.partial(_attention_kernel, nblk=nblk),
        grid=(bsz, nblk),
        in_specs=[pl.BlockSpec((1, ATT_BLOCK, d), lambda b, i: (b, i, 2)),
                  kv_spec(kblk, -1), kv_spec(kblk, 0), kv_spec(kblk, 1),
                  kv_spec(vblk, -1), kv_spec(vblk, 0), kv_spec(vblk, 1),
                  const((N_HEADS, ATT_BLOCK, 3 * ATT_BLOCK)), const((1, LANES)), const((1, LANES)),
                  pl.BlockSpec(memory_space=pltpu.SMEM)],
        out_specs=pl.BlockSpec((1, ATT_BLOCK, d), lambda b, i: (b, i, 0)),
        out_shape=jax.ShapeDtypeStruct((bsz, t, d), F32),
        compiler_params=_cparams("parallel", "parallel"),
        name="attention",
    )(u3, u3, u3, u3, u3, u3, u3, bias, q_norm_w, k_norm_w, sink)


def _t5_bucket(rel):
    nb = REL_BUCKETS // 2
    max_exact = nb // 2
    n = jnp.abs(rel)
    nf = jnp.maximum(n, 1).astype(jnp.float32)
    large = max_exact + (jnp.log(nf / max_exact) / math.log(REL_MAX_DIST / max_exact)
                         * (nb - max_exact)).astype(jnp.int32)
    large = jnp.minimum(large, nb - 1)
    return (rel > 0).astype(jnp.int32) * nb + jnp.where(n < max_exact, n, large)


def _attention_bias(rel_bias):
    rel = jnp.arange(3 * ATT_BLOCK)[None, :] - ATT_BLOCK - jnp.arange(ATT_BLOCK)[:, None]
    bias = rel_bias.astype(F32)[_t5_bucket(rel)].transpose(2, 0, 1)
    bias = jnp.where((jnp.abs(rel) <= ATT_BLOCK)[None], bias, -jnp.inf)
    return bias[jnp.asarray(_ATT_HEAD_ORDER)]


_ATT_HEAD_ORDER = tuple(8 * (j // 4) + 4 * s + (j % 4) for j in range(8) for s in range(2))


def _permute_heads(w, axis):
    shape = w.shape
    split = shape[:axis] + (N_HEADS, HEAD_DIM) + shape[axis + 1:]
    return jnp.take(w.reshape(split), jnp.asarray(_ATT_HEAD_ORDER), axis=axis).reshape(shape)


def _pad_cols(w, width):
    return jnp.pad(w, ((0, 0), (0, width - w.shape[1])))


def _even_in_weight(w_in):
    z = w_in[:, :D_MODEL]
    xbc = w_in[:, D_MODEL:D_MODEL + SSD_XBC]
    dt = w_in[:, D_MODEL + SSD_XBC:D_MODEL + SSD_XBC + 2 * N_HEADS]
    p = w_in[:, D_MODEL + SSD_XBC + 2 * N_HEADS:]
    w = jnp.concatenate([p[:, :EV_RKV_W], xbc, p[:, EV_RKV_W:], _pad_cols(dt, 2 * LANES), z], axis=1)
    assert w.shape[1] == EV_W
    return w.astype(BF16)


def _row(v):
    return v.reshape(1, -1).astype(F32)


def _even_params(e, ev_w_in, ev_w_out, ssd_conv_w, ssd_conv_b, ssd_dt_bias, ssd_a_log, ssd_d, ssd_norm_w,
                 rwkv_mu, rwkv_w0, rwkv_w2, rwkv_a0, rwkv_a2, rwkv_g2, rwkv_k_k, rwkv_k_a, rwkv_r_k,
                 rwkv_ln_w, rwkv_ln_b):
    zeros = lambda n: jnp.zeros((n, D_MODEL), F32)
    lr_pad = lambda w, lo: jnp.concatenate([zeros(lo), w.astype(F32), zeros(EV_LR_W - lo - w.shape[0])], 0).astype(BF16)
    return dict(
        w_in=_even_in_weight(ev_w_in[e]),
        w_out=ev_w_out[e].astype(BF16),
        conv_w=jnp.pad(ssd_conv_w[e].astype(F32), ((0, SUBLANES - SSD_CONV), (0, 0))),
        conv_b=_row(ssd_conv_b[e]),
        dt_bias=_pad_cols(_row(ssd_dt_bias[e]), LANES),
        a_log=_pad_cols(_row(ssd_a_log[e]), LANES),
        d_skip=_row(jnp.repeat(ssd_d[e], HEAD_DIM)),
        ssd_norm_w=_row(ssd_norm_w[e]),
        mu_rkv=_row(rwkv_mu[e, :EV_RKV_W]),
        mu_lr=_row(rwkv_mu[e, EV_RKV_W:]),
        a0=_row(rwkv_a0[e]),
        w0=rwkv_w0[e].reshape(2, 1, D_MODEL).astype(F32),
        w2=jnp.stack([lr_pad(rwkv_w2[e, d], 0) for d in range(2)]),
        a2=lr_pad(rwkv_a2[e], 64),
        g2=lr_pad(rwkv_g2[e], 128),
        k_k=_row(rwkv_k_k[e]), k_a=_row(rwkv_k_a[e]), r_k=_row(rwkv_r_k[e]),
        ln_w=_row(rwkv_ln_w[e]), ln_b=_row(rwkv_ln_b[e]),
    )


def _odd_params(w_in, w_out, dw_w, dw_b, ln_w, ln_b, q_norm_w, k_norm_w, sink):
    q_lo, q_hi = 2 * D_MODEL, 3 * D_MODEL
    w_in = jnp.concatenate([w_in[:, :q_lo], _permute_heads(w_in[:, q_lo:q_hi], 1), w_in[:, q_hi:]], axis=1)
    w_out = jnp.concatenate([w_out[:D_MODEL], _permute_heads(w_out[D_MODEL:], 0)], axis=0)
    return dict(
        w_in=w_in.astype(BF16), w_out=w_out.astype(BF16),
        dw_w=jnp.pad(dw_w.astype(F32), ((0, 4 * SUBLANES - CONV_WIDTH), (0, 0))),
        dw_b=_row(dw_b), ln_w=_row(ln_w), ln_b=_row(ln_b),
        q_w=_row(jnp.tile(q_norm_w, 2)), k_w=_row(jnp.tile(k_norm_w, 2)),
        sink=sink.astype(F32)[jnp.asarray(_ATT_HEAD_ORDER)],
    )


def _even_mixers(x3, norm_w, p):
    bsz, t, d = x3.shape
    x2 = x3.reshape(bsz * t, d)
    u3 = _norm_matmul(x2, norm_w, p["w_in"], min(bsz * t, NORM_MM_ROWS), 512).reshape(bsz, t, EV_W)
    a, lw0, lw1, g = _rwkv_prep(u3, p["mu_lr"], p["a0"], p["a2"], p["w0"], p["w2"], p["g2"], min(t, 256))
    yb_f, rkvs, ya_f, xa = _even_scan(
        _rwkv_fwd_part(u3, p["mu_rkv"], a, lw0, p["k_k"], p["k_a"]),
        _ssd_fwd_part(u3, p["conv_w"], p["conv_b"], p["dt_bias"], p["a_log"]), bsz, t, rev=False)
    yb, ya = _even_scan(
        _rwkv_bwd_part(rkvs, a, lw1, p["k_k"], p["k_a"], yb_f, g, p["r_k"], p["ln_w"], p["ln_b"]),
        _ssd_bwd_part(u3, xa, ya_f, p["dt_bias"], p["a_log"], p["d_skip"], p["ssd_norm_w"]), bsz, t, rev=True)
    return ya, yb


def _odd_mixers(x3, norm_w, p, bias):
    bsz, t, d = x3.shape
    x2 = x3.reshape(bsz * t, d)
    u3 = _norm_matmul(x2, norm_w, p["w_in"], min(bsz * t, NORM_MM_ROWS), 512).reshape(bsz, t, -1)
    yc = _conv_module(u3, p["dw_w"], p["dw_b"], p["ln_w"], p["ln_b"], min(t, 256))
    yd = _attention(u3, bias, p["q_w"], p["k_w"], p["sink"])
    return yc, yd


def _trunk(x3, layers, bias):
    bsz, t, d = x3.shape
    tm = min(bsz * t, 512)
    for kind, norm_mix, p, norm_ffn, ffn_wi, ffn_wo in layers:
        ya, yb = _even_mixers(x3, norm_mix, p) if kind == "even" else _odd_mixers(x3, norm_mix, p, bias)
        x3 = _proj_ffn(x3.reshape(bsz * t, d), ya.reshape(bsz * t, d), yb.reshape(bsz * t, d), p["w_out"],
                       norm_ffn, ffn_wi, ffn_wo, tm).reshape(bsz, t, d)
    return x3


def kernel(x_prompt, x_sample, rel_bias, norm_mix_w, norm_ffn_w, ffn_w_in, ffn_w_out, ev_w_in, ev_w_out, ssd_conv_w, ssd_conv_b, ssd_dt_bias, ssd_a_log, ssd_d, ssd_norm_w, rwkv_mu, rwkv_w0, rwkv_w2, rwkv_a0, rwkv_a2, rwkv_g2, rwkv_k_k, rwkv_k_a, rwkv_r_k, rwkv_ln_w, rwkv_ln_b, od_w_in, od_w_out, conv_dw_w, conv_dw_b, conv_ln_w, conv_ln_b, att_q_norm_w, att_k_norm_w, att_sink):
    depth = norm_mix_w.shape[0]
    bias = _attention_bias(rel_bias)
    layers = []
    for layer in range(depth):
        i = layer // 2
        if layer % 2 == 0:
            kind = "even"
            p = _even_params(i, ev_w_in, ev_w_out, ssd_conv_w, ssd_conv_b, ssd_dt_bias, ssd_a_log, ssd_d,
                             ssd_norm_w, rwkv_mu, rwkv_w0, rwkv_w2, rwkv_a0, rwkv_a2, rwkv_g2, rwkv_k_k,
                             rwkv_k_a, rwkv_r_k, rwkv_ln_w, rwkv_ln_b)
        else:
            kind = "odd"
            p = _odd_params(od_w_in[i], od_w_out[i], conv_dw_w[i], conv_dw_b[i], conv_ln_w[i], conv_ln_b[i],
                            att_q_norm_w[i], att_k_norm_w[i], att_sink[i])
        layers.append((kind, _row(norm_mix_w[layer]), p, _row(norm_ffn_w[layer]),
                       ffn_w_in[layer].astype(BF16), ffn_w_out[layer].astype(BF16)))
    return (_trunk(x_prompt, layers, bias), _trunk(x_sample, layers, bias))
```

```python
import functools
import math

import jax
import jax.numpy as jnp
import numpy as np
from jax import lax
from jax.experimental import pallas as pl
from jax.experimental.pallas import tpu as pltpu

F32 = jnp.float32
BF16 = jnp.bfloat16
HIGHEST = lax.Precision.HIGHEST

LANES = 128
SUBLANES = 8
VMEM_LIMIT_BYTES = 56 * 1024 * 1024

D_MODEL = 1024
HEAD_DIM = 64
D_FF = 2816
FF_CHUNK = 256
NORM_MM_ROWS = 2048
N_HEADS = 16
SSD_GROUPS = 2
SSD_STATE = 128
SSD_CONV = 5
SSD_XBC = D_MODEL + 2 * SSD_GROUPS * SSD_STATE
CHUNK = 128
SCAN_ROWS = 2 * CHUNK
RWKV_LN_EPS = 64e-5
CONV_WIDTH = 31
CONV_HALO = 16
ATT_BLOCK = 128
ATT_KV_HEADS = 4
ATT_GQ = 4
REL_BUCKETS = 32
REL_MAX_DIST = 128

EV_W = 6144
EV_RKV_W, EV_XBC_W, EV_LR_W, EV_DT_W, EV_Z_W = 3072, SSD_XBC, 256, LANES, D_MODEL
EV_XBC_BLK = 3072 // EV_XBC_W
EV_LR_BLK = 4608 // EV_LR_W
EV_DT_BLK = 4864 // EV_DT_W
EV_Z_BLK = 5120 // EV_Z_W


def _cparams(*sem):
    return pltpu.CompilerParams(dimension_semantics=sem, vmem_limit_bytes=VMEM_LIMIT_BYTES)


def _mm(a, b):
    return jnp.dot(a.astype(BF16), b.astype(BF16), preferred_element_type=F32)


def _mm_nt(a, b):
    return lax.dot_general(a.astype(BF16), b.astype(BF16), (((1,), (1,)), ((), ())),
                           preferred_element_type=F32)


def _mm_tn(a, b):
    return lax.dot_general(a.astype(BF16), b.astype(BF16), (((0,), (0,)), ((), ())),
                           preferred_element_type=F32)


def _rms(x, w, eps=1e-6):
    return x * lax.rsqrt(jnp.mean(x * x, axis=-1, keepdims=True) + eps) * w


def _norm_matmul_kernel(x_ref, nw_ref, w_ref, o_ref, h_ref):
    @pl.when(pl.program_id(1) == 0)
    def _():
        h_ref[...] = _rms(x_ref[...], nw_ref[...]).astype(BF16)

    o_ref[...] = jnp.dot(h_ref[...], w_ref[...], preferred_element_type=F32)


def _norm_matmul(x2, nw, w, tm, tn):
    n, d = x2.shape
    nout = w.shape[1]
    return pl.pallas_call(
        _norm_matmul_kernel,
        grid=(n // tm, nout // tn),
        in_specs=[pl.BlockSpec((tm, d), lambda i, j: (i, 0)),
                  pl.BlockSpec((1, d), lambda i, j: (0, 0)),
                  pl.BlockSpec((d, tn), lambda i, j: (0, j))],
        out_specs=pl.BlockSpec((tm, tn), lambda i, j: (i, j)),
        out_shape=jax.ShapeDtypeStruct((n, nout), F32),
        scratch_shapes=[pltpu.VMEM((tm, d), BF16)],
        compiler_params=_cparams("parallel", "arbitrary"),
        name="norm_matmul",
    )(x2, nw, w)


def _proj_ffn_kernel(x_ref, a_ref, b_ref, wm_ref, nw_ref, wi_ref, wo_ref, o_ref):
    d = x_ref.shape[1]
    x = x_ref[...] + _mm(a_ref[...], wm_ref[0:d, :]) + _mm(b_ref[...], wm_ref[d:2 * d, :])
    h = _rms(x, nw_ref[...]).astype(BF16)
    o_ref[...] = x
    for c in range(D_FF // FF_CHUNK):
        lo = c * FF_CHUNK
        g = jnp.dot(h, wi_ref[:, lo:lo + FF_CHUNK], preferred_element_type=F32)
        u = jnp.dot(h, wi_ref[:, D_FF + lo:D_FF + lo + FF_CHUNK], preferred_element_type=F32)
        act = (g * jax.nn.sigmoid(g) * u).astype(BF16)
        o_ref[...] += jnp.dot(act, wo_ref[lo:lo + FF_CHUNK, :], preferred_element_type=F32)


def _proj_ffn(x2, a2, b2, w_mix, nw, wi, wo, tm):
    n, d = x2.shape
    row = pl.BlockSpec((tm, d), lambda i: (i, 0))
    resident = lambda shape: pl.BlockSpec(shape, lambda i: (0, 0), pipeline_mode=pl.Buffered(1))
    return pl.pallas_call(
        _proj_ffn_kernel,
        grid=(n // tm,),
        in_specs=[row, row, row, resident((2 * d, d)), pl.BlockSpec((1, d), lambda i: (0, 0)),
                  resident((d, 2 * D_FF)), resident((D_FF, d))],
        out_specs=row,
        out_shape=jax.ShapeDtypeStruct((n, d), F32),
        compiler_params=_cparams("parallel"),
        name="proj_ffn",
    )(x2, a2, b2, w_mix, nw, wi, wo)


def _shift_mix(p, prev_row, next_row, mu):
    tt = p.shape[0]
    row8 = lax.broadcasted_iota(jnp.int32, (SUBLANES, p.shape[1]), 0)
    prev = pltpu.roll(p, 1, 0)
    prev = jnp.concatenate([jnp.where(row8 == 0, prev_row, prev[:SUBLANES]), prev[SUBLANES:]], axis=0)
    nxt = pltpu.roll(p, tt - 1, 0)
    nxt = jnp.concatenate([nxt[:tt - SUBLANES], jnp.where(row8 == SUBLANES - 1, next_row, nxt[tt - SUBLANES:])],
                          axis=0)
    return p + mu * (0.5 * (prev + nxt) - p)


def _rwkv_prep_kernel(lr_ref, lr_p_ref, lr_n_ref, mu_lr_ref, a0_ref, a2_ref, w0_ref, w2_ref, g2_ref,
                      a_ref, lw0_ref, lw1_ref, g_ref, *, nt):
    t = pl.program_id(1)
    keep_p = (t > 0).astype(F32)
    keep_n = (t < nt - 1).astype(F32)
    lr = _shift_mix(lr_ref[0], lr_p_ref[0, 7:8, :] * keep_p, lr_n_ref[0, 0:1, :] * keep_n, mu_lr_ref[...])
    a_ref[0] = jax.nn.sigmoid(a0_ref[...] + _mm(lr, a2_ref[...]))
    g_ref[0] = _mm(jax.nn.sigmoid(lr), g2_ref[...])
    th = jnp.tanh(lr)
    for d, out in enumerate((lw0_ref, lw1_ref)):
        wlog = -jax.nn.softplus(-(w0_ref[d] + _mm(th, w2_ref[d]))) - 0.5
        out[0] = -jnp.exp(wlog)


def _rwkv_prep(u3, mu_lr, a0, a2p, w0, w2p, g2p, tt):
    bsz, t, _ = u3.shape
    nt = t // tt
    r8 = tt // SUBLANES
    last8 = t // SUBLANES - 1
    d = D_MODEL

    def const(shape):
        return pl.BlockSpec(shape, lambda b, i: (0,) * len(shape))

    out = pl.BlockSpec((1, tt, d), lambda b, i: (b, i, 0))
    sds = jax.ShapeDtypeStruct((bsz, t, d), F32)
    return pl.pallas_call(
        functools.partial(_rwkv_prep_kernel, nt=nt),
        grid=(bsz, nt),
        in_specs=[pl.BlockSpec((1, tt, EV_LR_W), lambda b, i: (b, i, EV_LR_BLK)),
                  pl.BlockSpec((1, SUBLANES, EV_LR_W), lambda b, i: (b, jnp.maximum(i * r8 - 1, 0), EV_LR_BLK)),
                  pl.BlockSpec((1, SUBLANES, EV_LR_W), lambda b, i: (b, jnp.minimum((i + 1) * r8, last8), EV_LR_BLK)),
                  const((1, EV_LR_W)), const((1, d)), const((EV_LR_W, d)),
                  const((2, 1, d)), const((2, EV_LR_W, d)), const((EV_LR_W, d))],
        out_specs=[out, out, out, out],
        out_shape=[sds, sds, sds, sds],
        compiler_params=_cparams("parallel", "parallel"),
        name="rwkv_prep",
    )(u3, u3, u3, mu_lr, a0, a2p, w0, w2p, g2p)


def _head_sum(x, lane_lo):
    s0 = jnp.sum(jnp.where(lane_lo, x, 0.0), axis=-1, keepdims=True)
    s1 = jnp.sum(jnp.where(lane_lo, 0.0, x), axis=-1, keepdims=True)
    return jnp.where(lane_lo, s0, s1)


def _same_block(ti, si, log2_size):
    return jnp.right_shift(ti, log2_size) == jnp.right_shift(si, log2_size)


def _unit_lower_inverses(a_list, ti, si, eye):
    sh = 4
    n = a_list[0].shape[0]
    mm = lambda a, b: jnp.dot(a, b, preferred_element_type=F32)
    x = [jnp.where(_same_block(ti, si, sh), a, 0.0) for a in a_list]
    t_inv = [eye + xi for xi in x]
    t_bf = [t.astype(BF16) for t in t_inv]
    x = [xi.astype(BF16) for xi in x]
    x = [mm(xi, xi).astype(BF16) for xi in x]
    yield
    for step in range(sh - 1):
        last = step == sh - 2
        prod = [mm(t if last else jnp.concatenate([t, xi], axis=0), xi) for t, xi in zip(t_bf, x)]
        t_inv = [t + p[:n] for t, p in zip(t_inv, prod)]
        t_bf = [t.astype(BF16) for t in t_inv]
        x = [p[n:].astype(BF16) for p in prod]
        yield
    while (1 << sh) < a_list[0].shape[0]:
        off = jnp.logical_and(jnp.logical_not(_same_block(ti, si, sh)), _same_block(ti, si, sh + 1))
        at = [mm(jnp.where(off, a, 0.0).astype(BF16), t).astype(BF16) for a, t in zip(a_list, t_bf)]
        yield
        t_inv = [t + mm(tb, m) for t, tb, m in zip(t_inv, t_bf, at)]
        t_bf = [t.astype(BF16) for t in t_inv]
        yield
        sh += 1
    return t_bf


def _cumsum_rows(tri_bf16, x):
    hi = x.astype(BF16)
    lo = (x - hi.astype(F32)).astype(BF16)
    return (jnp.dot(tri_bf16, hi, preferred_element_type=F32)
            + jnp.dot(tri_bf16, lo, preferred_element_type=F32))


def _rwkv_scan_body(refs, rev, nc):
    final = rev
    pairs = D_MODEL // LANES
    if final:
        (r_ref, k_ref, v_ref, a_ref, lw_ref, kk_w_ref, ka_w_ref,
         yf_ref, g_ref, rk_w_ref, lnw_ref, lnb_ref, o_ref, s_ref) = refs
    else:
        (r_ref, k_ref, v_ref, rp_ref, kp_ref, vp_ref, rn_ref, kn_ref, vn_ref, mu_ref,
         a_ref, lw_ref, kk_w_ref, ka_w_ref, o_ref, rkvs_ref, s_ref) = refs
    L = CHUNK
    c_id = pl.program_id(1)

    if final:
        r_all, k_all, v_all = r_ref[0], k_ref[0], v_ref[0]
    else:
        cc = nc - 1 - c_id if rev else c_id
        keep_p = (cc > 0).astype(F32)
        keep_n = (cc < nc - 1).astype(F32)
        mixed = []
        for i, (m_ref, p_ref, n_ref) in enumerate(((r_ref, rp_ref, rn_ref), (k_ref, kp_ref, kn_ref),
                                                   (v_ref, vp_ref, vn_ref))):
            x = _shift_mix(m_ref[0], p_ref[0, SUBLANES - 1:SUBLANES, :] * keep_p, n_ref[0, 0:1, :] * keep_n,
                           mu_ref[:, i * D_MODEL:(i + 1) * D_MODEL])
            rkvs_ref[0, :, i * D_MODEL:(i + 1) * D_MODEL] = x
            mixed.append(x)
        r_all, k_all, v_all = mixed
    yield

    lane_lo = lax.broadcasted_iota(jnp.int32, (L, LANES), 1) < HEAD_DIM
    ti = lax.broadcasted_iota(jnp.int32, (L, L), 0)
    si = lax.broadcasted_iota(jnp.int32, (L, L), 1)
    tri = (si >= ti) if rev else (si <= ti)
    strict = (si > ti) if rev else (si < ti)
    eye = (si == ti).astype(F32)
    tri_bf16 = jnp.where(tri, 1.0, 0.0).astype(BF16)
    own = lambda h, x: jnp.where(lane_lo, x, 0.0) if h == 0 else jnp.where(lane_lo, 0.0, x)
    subs = list(range(r_all.shape[0] // L))
    if rev:
        subs.reverse()
    units = [(j, q) for j in subs for q in range(pairs)]
    heads = [(n, h) for n in range(len(units)) for h in range(2)]
    rows = lambda j: slice(j * L, (j + 1) * L)
    cols = lambda q: slice(q * LANES, (q + 1) * LANES)

    lw_all = lw_ref[0]
    cum_all = {j: _cumsum_rows(tri_bf16, lw_all[rows(j)]) for j in subs}
    r, v, kmod, a_t, b_t, k_t, r_t, b_h, k_h, tot = [], [], [], [], [], [], [], [], [], []
    for n, (j, q) in enumerate(units):
        rj, c = rows(j), cols(q)
        k, a, cum, lw = k_all[rj, c], a_ref[0, rj, c], cum_all[j][:, c], lw_all[rj, c]
        kk = k * kk_w_ref[:, c]
        kk = kk / jnp.maximum(jnp.sqrt(_head_sum(kk * kk, lane_lo)), 1e-12)
        km = k * (1.0 + (a - 1.0) * ka_w_ref[:, c])
        ib = kk * a
        tq = cum[0:1, :] if rev else cum[L - 1:L, :]
        p_inv = jnp.exp(-cum)
        p_end = jnp.exp(tq - cum)
        r.append(r_all[rj, c])
        v.append(v_all[rj, c])
        kmod.append(km)
        tot.append(tq)
        a_t.append(-kk * jnp.exp(cum - lw))
        b_t.append(ib * p_inv)
        k_t.append(km * p_inv)
        r_t.append(r[n] * jnp.exp(cum))
        b_h.append(ib * p_end)
        k_h.append(km * p_end)
        if n % 2 == 1:
            yield

    a_m = [own(h, a_t[n]).astype(BF16) for n, h in heads]
    gram = [_mm_nt(jnp.concatenate([a_m[i], own(h, r_t[n]).astype(BF16)], axis=0),
                   jnp.concatenate([b_t[n], k_t[n]], axis=0)) for i, (n, h) in enumerate(heads)]
    yield
    a_ab = [jnp.where(strict, g[:L, :L], 0.0) for g in gram]
    av = [_mm(jnp.where(strict, g[:L, L:], 0.0), own(h, v[n])).astype(BF16) for g, (n, h) in zip(gram, heads)]
    m_rb = [jnp.where(tri, g[L:, :L], 0.0).astype(BF16) for g in gram]
    m_rk = [jnp.where(tri, g[L:, L:], 0.0).astype(BF16) for g in gram]
    yield
    t_inv = yield from _unit_lower_inverses(a_ab, ti, si, eye)

    w12 = [_mm(jnp.concatenate([t_inv[2 * n], t_inv[2 * n + 1]], axis=1),
               jnp.concatenate([jnp.concatenate([a_m[2 * n], av[2 * n].astype(BF16)], axis=1),
                                jnp.concatenate([a_m[2 * n + 1], av[2 * n + 1].astype(BF16)], axis=1)], axis=0))
           for n in range(len(units))]
    yield
    state = [s_ref[q] for q in range(pairs)]
    y = [None] * len(units)
    for i in range(len(subs)):
        ns = range(i * pairs, (i + 1) * pairs)
        state_bf = [s.astype(BF16) for s in state]
        u = [_mm_nt(w12[n][:, :LANES], state_bf[n % pairs]) + w12[n][:, LANES:] for n in ns]
        yield
        for n, un in zip(ns, u):
            u_lo, v_lo = own(0, un), own(0, v[n])
            uv = jnp.concatenate([u_lo, un - u_lo, v_lo, v[n] - v_lo], axis=0)
            m_cat = jnp.concatenate([m_rb[2 * n], m_rb[2 * n + 1], m_rk[2 * n], m_rk[2 * n + 1]], axis=1)
            y[n] = _mm_nt(r_t[n], state_bf[n % pairs]) + _mm(m_cat, uv)
        yield
        new_state = []
        for n, un in zip(ns, u):
            s_new = state[n % pairs] * jnp.exp(tot[n]) + _mm_tn(jnp.concatenate([un, v[n]], axis=0),
                                                                jnp.concatenate([b_h[n], k_h[n]], axis=0))
            new_state.append(jnp.where(_same_block(ti, si, 6), s_new, 0.0))
        state = new_state
        yield
    for q in range(pairs):
        s_ref[q] = state[q]

    for n, (j, q) in enumerate(units):
        rj, c = rows(j), cols(q)
        if not final:
            o_ref[0, rj, c] = y[n]
        else:
            yq = y[n] + yf_ref[0, rj, c]
            inv_n = 1.0 / HEAD_DIM
            yc = yq - _head_sum(yq, lane_lo) * inv_n
            yn = yc * lax.rsqrt(_head_sum(yc * yc, lane_lo) * inv_n + RWKV_LN_EPS)
            bonus = _head_sum(r[n] * kmod[n] * rk_w_ref[:, c], lane_lo)
            o_ref[0, rj, c] = (yn * lnw_ref[:, c] + lnb_ref[:, c] + bonus * v[n]) * g_ref[0, rj, c]


class _ScanPart:
    def __init__(self, in_specs, args, out_specs, out_shape, scratch):
        self.in_specs, self.args, self.out_specs, self.out_shape, self.scratch = (
            list(in_specs), list(args), list(out_specs), list(out_shape), list(scratch))


def _chunk_specs(t, rev):
    nc = t // SCAN_ROWS
    r8 = SCAN_ROWS // SUBLANES
    last8 = t // SUBLANES - 1
    cmap = (lambda c: nc - 1 - c) if rev else (lambda c: c)
    blk = lambda w, j: pl.BlockSpec((1, SCAN_ROWS, w), lambda b, c: (b, cmap(c), j))
    prev = lambda w, j: pl.BlockSpec((1, SUBLANES, w), lambda b, c: (b, jnp.maximum(cmap(c) * r8 - 1, 0), j))
    nxt = lambda w, j: pl.BlockSpec((1, SUBLANES, w), lambda b, c: (b, jnp.minimum((cmap(c) + 1) * r8, last8), j))
    const = lambda shape: pl.BlockSpec(shape, lambda b, c: (0,) * len(shape))
    return blk, prev, nxt, const


def _rwkv_fwd_part(u3, mu_rkv, a, lw, kk_w, ka_w):
    bsz, t, _ = u3.shape
    d = D_MODEL
    blk, prev, nxt, const = _chunk_specs(t, False)
    return _ScanPart(
        [blk(d, 0), blk(d, 1), blk(d, 2), prev(d, 0), prev(d, 1), prev(d, 2), nxt(d, 0), nxt(d, 1), nxt(d, 2),
         const((1, EV_RKV_W)), blk(d, 0), blk(d, 0), const((1, d)), const((1, d))],
        [u3] * 9 + [mu_rkv, a, lw, kk_w, ka_w],
        [blk(d, 0), blk(EV_RKV_W, 0)],
        [jax.ShapeDtypeStruct((bsz, t, d), F32), jax.ShapeDtypeStruct((bsz, t, EV_RKV_W), F32)],
        [pltpu.VMEM((d // LANES, LANES, LANES), F32)])


def _rwkv_bwd_part(rkvs, a, lw, kk_w, ka_w, yf, g, rk_w, ln_w, ln_b):
    bsz, t, _ = rkvs.shape
    d = D_MODEL
    blk, _, _, const = _chunk_specs(t, True)
    par = const((1, d))
    return _ScanPart(
        [blk(d, 0), blk(d, 1), blk(d, 2), blk(d, 0), blk(d, 0), par, par, blk(d, 0), blk(d, 0), par, par, par],
        [rkvs, rkvs, rkvs, a, lw, kk_w, ka_w, yf, g, rk_w, ln_w, ln_b],
        [blk(d, 0)], [jax.ShapeDtypeStruct((bsz, t, d), F32)],
        [pltpu.VMEM((d // LANES, LANES, LANES), F32)])


_DONE = object()


def _even_scan_kernel(*refs, rev, nc, counts):
    groups, pos = [], 0
    for n in counts:
        groups.append(list(refs[pos:pos + n]))
        pos += n
    rwkv_in, ssd_in, rwkv_out, ssd_out, rwkv_scr, ssd_scr = groups

    @pl.when(pl.program_id(1) == 0)
    def _():
        rwkv_scr[0][...] = jnp.zeros_like(rwkv_scr[0])
        ssd_scr[0][...] = jnp.zeros_like(ssd_scr[0])

    bodies = [_rwkv_scan_body(rwkv_in + rwkv_out + rwkv_scr, rev, nc),
              _ssd_body(ssd_in + ssd_out + ssd_scr, rev, nc)]
    while bodies:
        for body in list(bodies):
            if next(body, _DONE) is _DONE:
                bodies.remove(body)


def _even_scan(rwkv, ssd, bsz, t, rev):
    counts = (len(rwkv.in_specs), len(ssd.in_specs), len(rwkv.out_specs), len(ssd.out_specs),
              len(rwkv.scratch), len(ssd.scratch))
    return pl.pallas_call(
        functools.partial(_even_scan_kernel, rev=rev, nc=t // SCAN_ROWS, counts=counts),
        grid=(bsz, t // SCAN_ROWS),
        in_specs=rwkv.in_specs + ssd.in_specs,
        out_specs=rwkv.out_specs + ssd.out_specs,
        out_shape=rwkv.out_shape + ssd.out_shape,
        scratch_shapes=rwkv.scratch + ssd.scratch,
        compiler_params=_cparams("parallel", "arbitrary"),
        name="even_scan_bwd" if rev else "even_scan_fwd",
    )(*(rwkv.args + ssd.args))


def _split_dot(x, w_bf16, terms, lhs=False):
    out, rem = None, x
    for _ in range(terms):
        piece = rem.astype(BF16)
        rem = rem - piece.astype(F32)
        part = (jnp.dot(w_bf16, piece, preferred_element_type=F32) if lhs
                else jnp.dot(piece, w_bf16, preferred_element_type=F32))
        out = part if out is None else out + part
    return out


def _ssd_body(refs, rev, nc):
    if rev:
        (xa_ref, dt_ref, dtb_ref, alog_ref, ex_ref, z_ref, yf_ref, dsk_ref, nw_ref, o_ref, h_ref) = refs
    else:
        (xbc_ref, xp_ref, xn_ref, dt_ref, cw_ref, cb_ref, dtb_ref, alog_ref, ex_ref,
         o_ref, xa_out_ref, h_ref, pad_ref) = refs
    L = CHUNK
    c = pl.program_id(1)
    cc = nc - 1 - c if rev else c
    gw = D_MODEL // SSD_GROUPS
    nrows = dt_ref.shape[1]
    subs = list(range(nrows // L))
    if rev:
        subs.reverse()
    rows = lambda j: slice(j * L, (j + 1) * L)

    if rev:
        xa = xa_ref[0]
    else:
        pad_ref[0:SUBLANES, :] = xp_ref[0] * (cc > 0).astype(F32)
        pad_ref[SUBLANES:SUBLANES + nrows, :] = xbc_ref[0]
        pad_ref[SUBLANES + nrows:2 * SUBLANES + nrows, :] = xn_ref[0] * (cc < nc - 1).astype(F32)
        pieces = []
        for lo_c in range(0, EV_XBC_W, gw):
            cs_ = slice(lo_c, lo_c + gw)
            acc = cb_ref[:, cs_] + cw_ref[0:1, cs_] * pad_ref[SUBLANES - 2:SUBLANES - 2 + nrows, cs_]
            for j in range(1, SSD_CONV):
                lo = SUBLANES - 2 + j
                acc = acc + cw_ref[j:j + 1, cs_] * pad_ref[lo:lo + nrows, cs_]
            piece = acc * jax.nn.sigmoid(acc)
            xa_out_ref[0, :, cs_] = piece
            pieces.append(piece)
            yield
        xa = jnp.concatenate(pieces, axis=1)
    n_bc = SSD_GROUPS * SSD_STATE

    dtv_all = jax.nn.softplus(dt_ref[0] + dtb_ref[...])
    da_all = dtv_all * -jnp.exp(alog_ref[...])
    ti = lax.broadcasted_iota(jnp.int32, (L, L), 0)
    si = lax.broadcasted_iota(jnp.int32, (L, L), 1)
    tri = (si >= ti) if rev else (si <= ti)
    tri_bf16 = jnp.where(tri, 1.0, 0.0).astype(BF16)
    lane_lo = lax.broadcasted_iota(jnp.int32, (L, LANES), 1) < HEAD_DIM
    edge = 0 if rev else L - 1
    off = N_HEADS if rev else 0
    ex = ex_ref[...]
    yield

    pre = {}
    for j in subs:
        xs = xa[rows(j), :D_MODEL]
        cs = _split_dot(da_all[rows(j)], tri_bf16, 3, lhs=True)
        dt_e = _split_dot(dtv_all[rows(j)], ex, 2)
        cs_e = _split_dot(cs, ex, 3)
        yield
        tot_e = cs_e[edge:edge + 1, :]
        xd = xs * dt_e
        xst = xd * jnp.exp(tot_e - cs_e)
        ecs = jnp.exp(cs_e)
        bm = [xa[rows(j), D_MODEL + g * SSD_STATE:D_MODEL + (g + 1) * SSD_STATE] for g in range(SSD_GROUPS)]
        cm = [xa[rows(j), D_MODEL + n_bc + g * SSD_STATE:D_MODEL + n_bc + (g + 1) * SSD_STATE]
              for g in range(SSD_GROUPS)]
        cb = [_mm_nt(cm[g], bm[g]) for g in range(SSD_GROUPS)]
        st = [_mm(bm[g].T, xst[:, g * gw:(g + 1) * gw]) for g in range(SSD_GROUPS)]
        yield
        cs_t = cs.T
        y_diag = []
        for p in range(N_HEADS // 2):
            gmat = []
            for hh in (2 * p, 2 * p + 1):
                col = cs[:, off + hh:off + hh + 1]
                row = cs_t[off + hh:off + hh + 1, :]
                decay = jnp.exp(jnp.where(tri, col - row, -jnp.inf))
                gmat.append((cb[hh // (N_HEADS // SSD_GROUPS)] * decay).astype(BF16))
            xp = xd[:, p * LANES:(p + 1) * LANES]
            x_lo = jnp.where(lane_lo, xp, 0.0)
            rhs = jnp.concatenate([x_lo, xp - x_lo], axis=0)
            y_diag.append(_mm(jnp.concatenate(gmat, axis=1), rhs))
            yield
        pre[j] = (xs, cm, st, tot_e, ecs, jnp.concatenate(y_diag, axis=1))

    state = [h_ref[g] for g in range(SSD_GROUPS)]
    for j in subs:
        xs, cm, st, tot_e, ecs, y_diag = pre[j]
        y_off = [_mm(cm[g], state[g]) for g in range(SSD_GROUPS)]
        state = [state[g] * jnp.exp(tot_e[:, g * gw:(g + 1) * gw]) + st[g] for g in range(SSD_GROUPS)]
        y = y_diag + jnp.concatenate(y_off, axis=1) * ecs
        if not rev:
            o_ref[0, rows(j), :] = y
        else:
            z = z_ref[0, rows(j), :]
            y = (y + yf_ref[0, rows(j), :] + dsk_ref[...] * xs) * (z * jax.nn.sigmoid(z))
            parts = []
            for g in range(SSD_GROUPS):
                yg = y[:, g * gw:(g + 1) * gw]
                parts.append(yg * lax.rsqrt(jnp.mean(yg * yg, axis=-1, keepdims=True) + 1e-6))
            o_ref[0, rows(j), :] = jnp.concatenate(parts, axis=1) * nw_ref[...]
        yield
    for g in range(SSD_GROUPS):
        h_ref[g] = state[g]


def _head_expand_matrix(rev):
    e = np.zeros((LANES, D_MODEL), np.float32)
    for hh in range(N_HEADS):
        e[(N_HEADS if rev else 0) + hh, hh * HEAD_DIM:(hh + 1) * HEAD_DIM] = 1.0
    return jnp.asarray(e, dtype=BF16)


def _ssd_fwd_part(u3, conv_w, conv_b, dt_bias, a_log):
    bsz, t, _ = u3.shape
    blk, prev, nxt, const = _chunk_specs(t, False)
    return _ScanPart(
        [blk(EV_XBC_W, EV_XBC_BLK), prev(EV_XBC_W, EV_XBC_BLK), nxt(EV_XBC_W, EV_XBC_BLK), blk(EV_DT_W, EV_DT_BLK),
         const((SUBLANES, EV_XBC_W)), const((1, EV_XBC_W)), const((1, LANES)), const((1, LANES)),
         const((LANES, D_MODEL))],
        [u3, u3, u3, u3, conv_w, conv_b, dt_bias, a_log, _head_expand_matrix(False)],
        [blk(D_MODEL, 0), blk(EV_XBC_W, 0)],
        [jax.ShapeDtypeStruct((bsz, t, D_MODEL), F32), jax.ShapeDtypeStruct((bsz, t, EV_XBC_W), F32)],
        [pltpu.VMEM((SSD_GROUPS, SSD_STATE, D_MODEL // SSD_GROUPS), F32),
         pltpu.VMEM((SCAN_ROWS + 2 * SUBLANES, EV_XBC_W), F32)])


def _ssd_bwd_part(u3, xa, yf, dt_bias, a_log, d_skip, norm_w):
    bsz, t, _ = u3.shape
    blk, _, _, const = _chunk_specs(t, True)
    return _ScanPart(
        [blk(EV_XBC_W, 0), blk(EV_DT_W, EV_DT_BLK), const((1, LANES)), const((1, LANES)), const((LANES, D_MODEL)),
         blk(EV_Z_W, EV_Z_BLK), blk(D_MODEL, 0), const((1, D_MODEL)), const((1, D_MODEL))],
        [xa, u3, dt_bias, a_log, _head_expand_matrix(True), u3, yf, d_skip, norm_w],
        [blk(D_MODEL, 0)], [jax.ShapeDtypeStruct((bsz, t, D_MODEL), F32)],
        [pltpu.VMEM((SSD_GROUPS, SSD_STATE, D_MODEL // SSD_GROUPS), F32)])


def _conv_module_kernel(val_ref, gate_ref, vp_ref, gp_ref, vn_ref, gn_ref,
                        dw_ref, db_ref, lnw_ref, lnb_ref, o_ref, pad_ref, sh_ref, *, nt):
    t = pl.program_id(1)
    tt = val_ref.shape[1]
    h = CONV_HALO
    pad_ref[0:h, :] = vp_ref[0] * jax.nn.sigmoid(gp_ref[0]) * (t > 0).astype(F32)
    pad_ref[h:h + tt, :] = val_ref[0] * jax.nn.sigmoid(gate_ref[0])
    pad_ref[h + tt:2 * h + tt, :] = vn_ref[0] * jax.nn.sigmoid(gn_ref[0]) * (t < nt - 1).astype(F32)
    rows = tt + 2 * h - SUBLANES
    for b in range(1, SUBLANES):
        sh_ref[b - 1] = pad_ref[b:b + rows, :]
    acc = db_ref[...]
    for j in range(CONV_WIDTH):
        a, b = divmod(h - CONV_WIDTH // 2 + j, SUBLANES)
        lo = a * SUBLANES
        tap = pad_ref[lo:lo + tt, :] if b == 0 else sh_ref[b - 1, lo:lo + tt, :]
        acc = acc + dw_ref[j:j + 1, :] * tap
    xc = acc - jnp.mean(acc, axis=-1, keepdims=True)
    xn = xc * lax.rsqrt(jnp.mean(xc * xc, axis=-1, keepdims=True) + 1e-5)
    yv = xn * lnw_ref[...] + lnb_ref[...]
    o_ref[0] = yv * jax.nn.sigmoid(yv)


def _conv_module(u3, dw_w, dw_b, ln_w, ln_b, tt):
    bsz, t, _ = u3.shape
    nt = t // tt
    rh = tt // CONV_HALO
    lasth = t // CONV_HALO - 1
    d = D_MODEL

    def const(shape):
        return pl.BlockSpec(shape, lambda b, i: (0,) * len(shape))

    main = lambda blk: pl.BlockSpec((1, tt, d), lambda b, i: (b, i, blk))
    prev = lambda blk: pl.BlockSpec((1, CONV_HALO, d), lambda b, i: (b, jnp.maximum(i * rh - 1, 0), blk))
    nxt = lambda blk: pl.BlockSpec((1, CONV_HALO, d), lambda b, i: (b, jnp.minimum((i + 1) * rh, lasth), blk))
    return pl.pallas_call(
        functools.partial(_conv_module_kernel, nt=nt),
        grid=(bsz, nt),
        in_specs=[main(0), main(1), prev(0), prev(1), nxt(0), nxt(1),
                  const((4 * SUBLANES, d)), const((1, d)), const((1, d)), const((1, d))],
        out_specs=pl.BlockSpec((1, tt, d), lambda b, i: (b, i, 0)),
        out_shape=jax.ShapeDtypeStruct((bsz, t, d), F32),
        scratch_shapes=[pltpu.VMEM((tt + 2 * CONV_HALO, d), F32),
                        pltpu.VMEM((SUBLANES - 1, tt + 2 * CONV_HALO - SUBLANES, d), F32)],
        compiler_params=_cparams("parallel", "parallel"),
        name="conv_module",
    )(u3, u3, u3, u3, u3, u3, dw_w, dw_b, ln_w, ln_b)


def _attention_kernel(q_ref, k0_ref, k1_ref, k2_ref, v0_ref, v1_ref, v2_ref,
                      bias_ref, qw_ref, kw_ref, sink_ref, o_ref, *, nblk):
    i = pl.program_id(1)
    blk = ATT_BLOCK
    nqb = D_MODEL // LANES
    col = lax.broadcasted_iota(jnp.int32, (blk, 3 * blk), 1)
    oob = ((i == 0) & (col < blk)) | ((i == nblk - 1) & (col >= 2 * blk))
    lane_lo = lax.broadcasted_iota(jnp.int32, (blk, LANES), 1) < HEAD_DIM
    lane_lo3 = lax.broadcasted_iota(jnp.int32, (3 * blk, LANES), 1) < HEAD_DIM
    scale = HEAD_DIM ** -0.5

    def head_rms(x, w, lo):
        return x * lax.rsqrt(_head_sum(x * x, lo) * (1.0 / HEAD_DIM) + 1e-6) * w

    k_all = jnp.concatenate([k0_ref[0], k1_ref[0], k2_ref[0]], axis=0)
    v_all = jnp.concatenate([v0_ref[0], v1_ref[0], v2_ref[0]], axis=0)
    kn = [head_rms(k_all[:, kb * LANES:(kb + 1) * LANES], kw_ref[...], lane_lo3) for kb in range(2)]
    qn = [head_rms(q_ref[0, :, j * LANES:(j + 1) * LANES], qw_ref[...], lane_lo) for j in range(nqb)]
    half = nqb // 2
    logits = []
    for kb in range(2):
        stack = []
        for j in range(kb * half, (kb + 1) * half):
            q_lo = jnp.where(lane_lo, qn[j], 0.0)
            stack += [q_lo, qn[j] - q_lo]
        logits.append(_mm_nt(jnp.concatenate(stack, axis=0), kn[kb]))
    probs, rden = [], []
    for pos in range(N_HEADS):
        kb, idx = divmod(pos, 2 * half)
        lg = logits[kb][idx * blk:(idx + 1) * blk] * scale + bias_ref[pos]
        lg = jnp.where(oob, -jnp.inf, lg)
        sink = sink_ref[pos]
        m = jnp.maximum(jnp.max(lg, axis=-1, keepdims=True), sink)
        p = jnp.exp(lg - m)
        rden.append(1.0 / (jnp.sum(p, axis=-1, keepdims=True) + jnp.exp(sink - m)))
        probs.append(p.astype(BF16))
    for j in range(nqb):
        vb = v_all[:, (j // half) * LANES:(j // half + 1) * LANES]
        v_lo = jnp.where(lane_lo3, vb, 0.0)
        pv = _mm(jnp.concatenate([probs[2 * j], probs[2 * j + 1]], axis=1), jnp.concatenate([v_lo, vb - v_lo], axis=0))
        o_ref[0, :, j * LANES:(j + 1) * LANES] = pv * jnp.where(lane_lo, rden[2 * j], rden[2 * j + 1])


def _attention(u3, bias, q_norm_w, k_norm_w, sink):
    bsz, t, _ = u3.shape
    nblk = t // ATT_BLOCK
    d = D_MODEL
    kvw = ATT_KV_HEADS * HEAD_DIM
    kblk = (2 * D_MODEL + d) // kvw
    vblk = kblk + 1

    def kv_spec(blk, off):
        return pl.BlockSpec((1, ATT_BLOCK, kvw), lambda b, i: (b, jnp.clip(i + off, 0, nblk - 1), blk))

    def const(shape):
        return pl.BlockSpec(shape, lambda b, i: (0,) * len(shape))

    return pl.pallas_call(
        functools.partial(_attention_kernel, nblk=nblk),
        grid=(bsz, nblk),
        in_specs=[pl.BlockSpec((1, ATT_BLOCK, d), lambda b, i: (b, i, 2)),
                  kv_spec(kblk, -1), kv_spec(kblk, 0), kv_spec(kblk, 1),
                  kv_spec(vblk, -1), kv_spec(vblk, 0), kv_spec(vblk, 1),
                  const((N_HEADS, ATT_BLOCK, 3 * ATT_BLOCK)), const((1, LANES)), const((1, LANES)),
                  pl.BlockSpec(memory_space=pltpu.SMEM)],
        out_specs=pl.BlockSpec((1, ATT_BLOCK, d), lambda b, i: (b, i, 0)),
        out_shape=jax.ShapeDtypeStruct((bsz, t, d), F32),
        compiler_params=_cparams("parallel", "parallel"),
        name="attention",
    )(u3, u3, u3, u3, u3, u3, u3, bias, q_norm_w, k_norm_w, sink)


def _t5_bucket(rel):
    nb = REL_BUCKETS // 2
    max_exact = nb // 2
    n = jnp.abs(rel)
    nf = jnp.maximum(n, 1).astype(jnp.float32)
    large = max_exact + (jnp.log(nf / max_exact) / math.log(REL_MAX_DIST / max_exact)
                         * (nb - max_exact)).astype(jnp.int32)
    large = jnp.minimum(large, nb - 1)
    return (rel > 0).astype(jnp.int32) * nb + jnp.where(n < max_exact, n, large)


def _attention_bias(rel_bias):
    rel = jnp.arange(3 * ATT_BLOCK)[None, :] - ATT_BLOCK - jnp.arange(ATT_BLOCK)[:, None]
    bias = rel_bias.astype(F32)[_t5_bucket(rel)].transpose(2, 0, 1)
    bias = jnp.where((jnp.abs(rel) <= ATT_BLOCK)[None], bias, -jnp.inf)
    return bias[jnp.asarray(_ATT_HEAD_ORDER)]


_ATT_HEAD_ORDER = tuple(8 * (j // 4) + 4 * s + (j % 4) for j in range(8) for s in range(2))


def _permute_heads(w, axis):
    shape = w.shape
    split = shape[:axis] + (N_HEADS, HEAD_DIM) + shape[axis + 1:]
    return jnp.take(w.reshape(split), jnp.asarray(_ATT_HEAD_ORDER), axis=axis).reshape(shape)


def _pad_cols(w, width):
    return jnp.pad(w, ((0, 0), (0, width - w.shape[1])))


def _even_in_weight(w_in):
    z = w_in[:, :D_MODEL]
    xbc = w_in[:, D_MODEL:D_MODEL + SSD_XBC]
    dt = w_in[:, D_MODEL + SSD_XBC:D_MODEL + SSD_XBC + 2 * N_HEADS]
    p = w_in[:, D_MODEL + SSD_XBC + 2 * N_HEADS:]
    w = jnp.concatenate([p[:, :EV_RKV_W], xbc, p[:, EV_RKV_W:], _pad_cols(dt, 2 * LANES), z], axis=1)
    assert w.shape[1] == EV_W
    return w.astype(BF16)


def _row(v):
    return v.reshape(1, -1).astype(F32)


def _even_params(e, ev_w_in, ev_w_out, ssd_conv_w, ssd_conv_b, ssd_dt_bias, ssd_a_log, ssd_d, ssd_norm_w,
                 rwkv_mu, rwkv_w0, rwkv_w2, rwkv_a0, rwkv_a2, rwkv_g2, rwkv_k_k, rwkv_k_a, rwkv_r_k,
                 rwkv_ln_w, rwkv_ln_b):
    zeros = lambda n: jnp.zeros((n, D_MODEL), F32)
    lr_pad = lambda w, lo: jnp.concatenate([zeros(lo), w.astype(F32), zeros(EV_LR_W - lo - w.shape[0])], 0).astype(BF16)
    return dict(
        w_in=_even_in_weight(ev_w_in[e]),
        w_out=ev_w_out[e].astype(BF16),
        conv_w=jnp.pad(ssd_conv_w[e].astype(F32), ((0, SUBLANES - SSD_CONV), (0, 0))),
        conv_b=_row(ssd_conv_b[e]),
        dt_bias=_pad_cols(_row(ssd_dt_bias[e]), LANES),
        a_log=_pad_cols(_row(ssd_a_log[e]), LANES),
        d_skip=_row(jnp.repeat(ssd_d[e], HEAD_DIM)),
        ssd_norm_w=_row(ssd_norm_w[e]),
        mu_rkv=_row(rwkv_mu[e, :EV_RKV_W]),
        mu_lr=_row(rwkv_mu[e, EV_RKV_W:]),
        a0=_row(rwkv_a0[e]),
        w0=rwkv_w0[e].reshape(2, 1, D_MODEL).astype(F32),
        w2=jnp.stack([lr_pad(rwkv_w2[e, d], 0) for d in range(2)]),
        a2=lr_pad(rwkv_a2[e], 64),
        g2=lr_pad(rwkv_g2[e], 128),
        k_k=_row(rwkv_k_k[e]), k_a=_row(rwkv_k_a[e]), r_k=_row(rwkv_r_k[e]),
        ln_w=_row(rwkv_ln_w[e]), ln_b=_row(rwkv_ln_b[e]),
    )


def _odd_params(w_in, w_out, dw_w, dw_b, ln_w, ln_b, q_norm_w, k_norm_w, sink):
    q_lo, q_hi = 2 * D_MODEL, 3 * D_MODEL
    w_in = jnp.concatenate([w_in[:, :q_lo], _permute_heads(w_in[:, q_lo:q_hi], 1), w_in[:, q_hi:]], axis=1)
    w_out = jnp.concatenate([w_out[:D_MODEL], _permute_heads(w_out[D_MODEL:], 0)], axis=0)
    return dict(
        w_in=w_in.astype(BF16), w_out=w_out.astype(BF16),
        dw_w=jnp.pad(dw_w.astype(F32), ((0, 4 * SUBLANES - CONV_WIDTH), (0, 0))),
        dw_b=_row(dw_b), ln_w=_row(ln_w), ln_b=_row(ln_b),
        q_w=_row(jnp.tile(q_norm_w, 2)), k_w=_row(jnp.tile(k_norm_w, 2)),
        sink=sink.astype(F32)[jnp.asarray(_ATT_HEAD_ORDER)],
    )


def _even_mixers(x3, norm_w, p):
    bsz, t, d = x3.shape
    x2 = x3.reshape(bsz * t, d)
    u3 = _norm_matmul(x2, norm_w, p["w_in"], min(bsz * t, NORM_MM_ROWS), 512).reshape(bsz, t, EV_W)
    a, lw0, lw1, g = _rwkv_prep(u3, p["mu_lr"], p["a0"], p["a2"], p["w0"], p["w2"], p["g2"], min(t, 256))
    yb_f, rkvs, ya_f, xa = _even_scan(
        _rwkv_fwd_part(u3, p["mu_rkv"], a, lw0, p["k_k"], p["k_a"]),
        _ssd_fwd_part(u3, p["conv_w"], p["conv_b"], p["dt_bias"], p["a_log"]), bsz, t, rev=False)
    yb, ya = _even_scan(
        _rwkv_bwd_part(rkvs, a, lw1, p["k_k"], p["k_a"], yb_f, g, p["r_k"], p["ln_w"], p["ln_b"]),
        _ssd_bwd_part(u3, xa, ya_f, p["dt_bias"], p["a_log"], p["d_skip"], p["ssd_norm_w"]), bsz, t, rev=True)
    return ya, yb


def _odd_mixers(x3, norm_w, p, bias):
    bsz, t, d = x3.shape
    x2 = x3.reshape(bsz * t, d)
    u3 = _norm_matmul(x2, norm_w, p["w_in"], min(bsz * t, NORM_MM_ROWS), 512).reshape(bsz, t, -1)
    yc = _conv_module(u3, p["dw_w"], p["dw_b"], p["ln_w"], p["ln_b"], min(t, 256))
    yd = _attention(u3, bias, p["q_w"], p["k_w"], p["sink"])
    return yc, yd


def _trunk(x3, layers, bias):
    bsz, t, d = x3.shape
    tm = min(bsz * t, 512)
    for kind, norm_mix, p, norm_ffn, ffn_wi, ffn_wo in layers:
        ya, yb = _even_mixers(x3, norm_mix, p) if kind == "even" else _odd_mixers(x3, norm_mix, p, bias)
        x3 = _proj_ffn(x3.reshape(bsz * t, d), ya.reshape(bsz * t, d), yb.reshape(bsz * t, d), p["w_out"],
                       norm_ffn, ffn_wi, ffn_wo, tm).reshape(bsz, t, d)
    return x3


def kernel(x_prompt, x_sample, rel_bias, norm_mix_w, norm_ffn_w, ffn_w_in, ffn_w_out, ev_w_in, ev_w_out, ssd_conv_w, ssd_conv_b, ssd_dt_bias, ssd_a_log, ssd_d, ssd_norm_w, rwkv_mu, rwkv_w0, rwkv_w2, rwkv_a0, rwkv_a2, rwkv_g2, rwkv_k_k, rwkv_k_a, rwkv_r_k, rwkv_ln_w, rwkv_ln_b, od_w_in, od_w_out, conv_dw_w, conv_dw_b, conv_ln_w, conv_ln_b, att_q_norm_w, att_k_norm_w, att_sink):
    depth = norm_mix_w.shape[0]
    bias = _attention_bias(rel_bias)
    layers = []
    for layer in range(depth):
        i = layer // 2
        if layer % 2 == 0:
            kind = "even"
            p = _even_params(i, ev_w_in, ev_w_out, ssd_conv_w, ssd_conv_b, ssd_dt_bias, ssd_a_log, ssd_d,
                             ssd_norm_w, rwkv_mu, rwkv_w0, rwkv_w2, rwkv_a0, rwkv_a2, rwkv_g2, rwkv_k_k,
                             rwkv_k_a, rwkv_r_k, rwkv_ln_w, rwkv_ln_b)
        else:
            kind = "odd"
            p = _odd_params(od_w_in[i], od_w_out[i], conv_dw_w[i], conv_dw_b[i], conv_ln_w[i], conv_ln_b[i],
                            att_q_norm_w[i], att_k_norm_w[i], att_sink[i])
        layers.append((kind, _row(norm_mix_w[layer]), p, _row(norm_ffn_w[layer]),
                       ffn_w_in[layer].astype(BF16), ffn_w_out[layer].astype(BF16)))
    return (_trunk(x_prompt, layers, bias), _trunk(x_sample, layers, bias))
```
